```python
import math, functools
import jax, jax.numpy as jnp
from jax import lax
import numpy as np

D_MODEL = 2048
BATCH = 16
SEQ = 2048
DEPTH = 4

GRID_W = 64
CTX_LEN = 256
FNET_GROUPS = 4
FNET_GROUP_W = 256
FNET_W = FNET_GROUPS * FNET_GROUP_W
DIFF_HEADS = 8
DIFF_HEAD_DIM = 64
DIFF_V_DIM = 2 * DIFF_HEAD_DIM
DIFF_QK_W = DIFF_HEADS * 2 * DIFF_HEAD_DIM
DIFF_V_W = DIFF_HEADS * DIFF_V_DIM
MLA_HEADS = 8
MLA_Q_LORA = 512
MLA_KV_LORA = 256
MLA_NOPE = 128
MLA_ROPE = 64
MLA_V = 128
MLA_QK = MLA_NOPE + MLA_ROPE
MLA_V_W = MLA_HEADS * MLA_V
N_BRANCH = 3
IN_COLS = FNET_W + 2 * DIFF_QK_W + DIFF_V_W + MLA_Q_LORA + MLA_KV_LORA + MLA_ROPE + N_BRANCH * D_MODEL
N_MOD = 6
D_FF_DENSE = 256 * ((8 * D_MODEL // 3 + 255) // 256)
N_EXPERTS = 8
TOP_K = 2
D_FF_EXPERT = 2 * D_MODEL
N_DENSE_LAYERS = (DEPTH + 1) // 2
N_MOE_LAYERS = DEPTH // 2
ROPE_THETA = 10000.0
Q_BLOCK = 128
EPS = 1e-6

kernel_name = 'hybrid_fourier_diffattn_mla_moe_dit_trunk'


def rms_norm(x, g):
    xf = x.astype(jnp.float32)
    y = xf * lax.rsqrt(jnp.mean(xf * xf, axis=-1, keepdims=True) + EPS)
    return (y * g.astype(jnp.float32)).astype(x.dtype)


def modulate(h, g, shift, scale):
    return rms_norm(h, g) * (1 + scale) + shift


def axial_rope_tables(n_tokens, rot_dim):
    rows = n_tokens // GRID_W
    row = jnp.repeat(jnp.arange(rows, dtype=jnp.float32), GRID_W)
    col = jnp.tile(jnp.arange(GRID_W, dtype=jnp.float32), rows)
    n_freq = rot_dim // 4
    inv_freq = ROPE_THETA ** (-jnp.arange(n_freq, dtype=jnp.float32) / n_freq)
    ang = jnp.concatenate([row[:, None] * inv_freq, col[:, None] * inv_freq], axis=-1)
    return jnp.cos(ang), jnp.sin(ang)


def apply_rope(x, cos, sin):
    bshape = (cos.shape[0],) + (1,) * (x.ndim - 3) + (cos.shape[-1],)
    cos = cos.reshape(bshape).astype(x.dtype)
    sin = sin.reshape(bshape).astype(x.dtype)
    x1, x2 = jnp.split(x, 2, axis=-1)
    return jnp.concatenate([x1 * cos - x2 * sin, x2 * cos + x1 * sin], axis=-1)


def split_in_proj(z):
    sizes = (FNET_W, DIFF_QK_W, DIFF_QK_W, DIFF_V_W, MLA_Q_LORA, MLA_KV_LORA, MLA_ROPE, N_BRANCH * D_MODEL)
    idx, acc = [], 0
    for s in sizes[:-1]:
        acc += s
        idx.append(acc)
    return jnp.split(z, idx, axis=-1)


def sweep_query_blocks(attend, q):
    b, n = q.shape[:2]
    nb = n // Q_BLOCK
    qb = jnp.moveaxis(q.reshape((b, nb, Q_BLOCK) + q.shape[2:]), 1, 0)
    out = jnp.moveaxis(lax.map(attend, qb), 0, 1)
    return out.reshape((b, n) + out.shape[3:])


def diff_attend(q, k, v, lam):
    s = jnp.einsum('bqhcd,bkhcd->bhcqk', q, k, preferred_element_type=jnp.float32) * (DIFF_HEAD_DIM ** -0.5)
    p = jax.nn.softmax(s, axis=-1)
    w = (p[:, :, 0] - lam * p[:, :, 1]).astype(v.dtype)
    return jnp.einsum('bhqk,bkhd->bqhd', w, v)


def mla_attend(q, k, v):
    s = jnp.einsum('bqhd,bkhd->bhqk', q, k, preferred_element_type=jnp.float32) * (MLA_QK ** -0.5)
    p = jax.nn.softmax(s, axis=-1).astype(v.dtype)
    return jnp.einsum('bhqk,bkhd->bqhd', p, v)


def diff_qkv(zq, zk, zv, g_q, g_k, rope):
    b, n = zq.shape[:2]
    q = rms_norm(zq.reshape(b, n, DIFF_HEADS, 2, DIFF_HEAD_DIM), g_q)
    k = rms_norm(zk.reshape(b, n, DIFF_HEADS, 2, DIFF_HEAD_DIM), g_k)
    if rope is not None:
        q = apply_rope(q, *rope)
        k = apply_rope(k, *rope)
    v = zv.reshape(b, n, DIFF_HEADS, DIFF_V_DIM)
    return q, k, v


def mla_qkv(zcq, zckv, zkr, lp, rope):
    b, n = zcq.shape[:2]
    q = (rms_norm(zcq, lp['g_mla_cq']) @ lp['w_mla_uq']).reshape(b, n, MLA_HEADS, MLA_QK)
    kv = (rms_norm(zckv, lp['g_mla_ckv']) @ lp['w_mla_ukv']).reshape(b, n, MLA_HEADS, MLA_NOPE + MLA_V)
    q_nope, q_rope = q[..., :MLA_NOPE], q[..., MLA_NOPE:]
    k_nope, v = kv[..., :MLA_NOPE], kv[..., MLA_NOPE:]
    k_rope = zkr
    if rope is not None:
        q_rope = apply_rope(q_rope, *rope)
        k_rope = apply_rope(k_rope, *rope)
    k_rope = jnp.broadcast_to(k_rope[:, :, None, :], (b, n, MLA_HEADS, MLA_ROPE))
    q = rms_norm(jnp.concatenate([q_nope, q_rope], axis=-1), lp['g_mla_q'])
    k = rms_norm(jnp.concatenate([k_nope, k_rope], axis=-1), lp['g_mla_k'])
    return q, k, v


def fourier_branch(zf, w_f):
    b, n = zf.shape[:2]
    zg = zf.reshape(b, n, FNET_GROUPS, FNET_GROUP_W).astype(jnp.float32)
    y = jnp.fft.fft2(zg, axes=(1, 3), norm='ortho').real.astype(zf.dtype)
    return y.reshape(b, n, FNET_W) @ w_f


def diff_branch_out(o, lp, lam_init):
    b, n = o.shape[:2]
    o = rms_norm(o, lp['g_diff_out']) * (1.0 - lam_init)
    return o.reshape(b, n, DIFF_V_W) @ lp['w_diff_o']


def mla_branch_out(o, lp):
    b, n = o.shape[:2]
    return o.reshape(b, n, MLA_V_W) @ lp['w_mla_o']


def merge_branches(gate_logits, f, d, m, w_out):
    b, n = f.shape[:2]
    g = jax.nn.sigmoid(gate_logits).reshape(b, n, N_BRANCH, D_MODEL)
    return (g[:, :, 0] * f + g[:, :, 1] * d + g[:, :, 2] * m) @ w_out


def token_mixers(uc, ul, lp, layer_idx, need_ctx, rope_diff, rope_mla):
    f_c, dq_c, dk_c, dv_c, cq_c, ckv_c, kr_c, gate_c = split_in_proj(uc @ lp['w_in'])
    f_l, dq_l, dk_l, dv_l, cq_l, ckv_l, kr_l, gate_l = split_in_proj(ul @ lp['w_in'])
    lam_init = 0.8 - 0.6 * math.exp(-0.3 * layer_idx)
    lam_p = lp['diff_lambda'].astype(jnp.float32)
    lam = jnp.exp(jnp.sum(lam_p[0] * lam_p[1])) - jnp.exp(jnp.sum(lam_p[2] * lam_p[3])) + lam_init
    qd_c, kd_c, vd_c = diff_qkv(dq_c, dk_c, dv_c, lp['g_diff_q'], lp['g_diff_k'], None)
    qm_c, km_c, vm_c = mla_qkv(cq_c, ckv_c, kr_c, lp, None)
    qd_l, kd_l, vd_l = diff_qkv(dq_l, dk_l, dv_l, lp['g_diff_q'], lp['g_diff_k'], rope_diff)
    qm_l, km_l, vm_l = mla_qkv(cq_l, ckv_l, kr_l, lp, rope_mla)
    kd_all = jnp.concatenate([kd_l, kd_c], axis=1)
    vd_all = jnp.concatenate([vd_l, vd_c], axis=1)
    km_all = jnp.concatenate([km_l, km_c], axis=1)
    vm_all = jnp.concatenate([vm_l, vm_c], axis=1)
    od_l = sweep_query_blocks(lambda qb: diff_attend(qb, kd_all, vd_all, lam), qd_l)
    om_l = sweep_query_blocks(lambda qb: mla_attend(qb, km_all, vm_all), qm_l)
    y_l = merge_branches(gate_l, fourier_branch(f_l, lp['w_fnet']), diff_branch_out(od_l, lp, lam_init),
                         mla_branch_out(om_l, lp), lp['w_out'])
    y_c = None
    if need_ctx:
        od_c = diff_attend(qd_c, kd_c, vd_c, lam)
        om_c = mla_attend(qm_c, km_c, vm_c)
        y_c = merge_branches(gate_c, fourier_branch(f_c, lp['w_fnet']), diff_branch_out(od_c, lp, lam_init),
                             mla_branch_out(om_c, lp), lp['w_out'])
    return y_c, y_l


def dense_swiglu(u, w1, w3, w2):
    return (jax.nn.silu(u @ w1) * (u @ w3)) @ w2


def moe_swiglu(u, w_router, w1, w3, w2):
    logits = (u @ w_router).astype(jnp.float32)
    top_v, top_i = lax.top_k(logits, TOP_K)
    top_p = jax.nn.softmax(top_v, axis=-1)
    gates = jnp.sum(jax.nn.one_hot(top_i, N_EXPERTS, dtype=jnp.float32) * top_p[..., None], axis=-2)
    out = jnp.zeros_like(u)
    for e in range(N_EXPERTS):
        h = jax.nn.silu(u @ w1[e]) * (u @ w3[e])
        out = out + gates[..., e:e + 1].astype(u.dtype) * (h @ w2[e])
    return out


def setup_inputs(seed: int = 0) -> dict:
    key = jax.random.key(seed)
    ks = jax.random.split(key, 30)
    f32 = jnp.float32
    D = D_MODEL

    def normal(k, shape, scale):
        return scale * jax.random.normal(k, shape, f32)

    def gain(k, shape):
        return 1.0 + 0.02 * jax.random.normal(k, shape, f32)

    return {
        'x': normal(ks[0], (BATCH, SEQ, D), 1.0),
        'c': normal(ks[1], (BATCH, D), 1.0),
        'ctx': normal(ks[2], (BATCH, CTX_LEN, D), 1.0),
        'c_ctx': normal(ks[3], (D,), 1.0),
        'w_mod': normal(ks[4], (DEPTH, D, N_MOD * D), D ** -0.5),
        'b_mod': normal(ks[5], (DEPTH, N_MOD * D), 0.01),
        'g_norm1': gain(ks[6], (DEPTH, D)),
        'g_norm2': gain(ks[7], (DEPTH, D)),
        'w_in': normal(ks[8], (DEPTH, D, IN_COLS), D ** -0.5),
        'g_diff_q': gain(ks[9], (DEPTH, DIFF_HEAD_DIM)),
        'g_diff_k': gain(ks[10], (DEPTH, DIFF_HEAD_DIM)),
        'diff_lambda': normal(ks[11], (DEPTH, 4, DIFF_HEAD_DIM), 0.1),
        'g_diff_out': gain(ks[12], (DEPTH, DIFF_V_DIM)),
        'w_diff_o': normal(ks[13], (DEPTH, DIFF_V_W, D), DIFF_V_W ** -0.5),
        'g_mla_cq': gain(ks[14], (DEPTH, MLA_Q_LORA)),
        'g_mla_ckv': gain(ks[15], (DEPTH, MLA_KV_LORA)),
        'w_mla_uq': normal(ks[16], (DEPTH, MLA_Q_LORA, MLA_HEADS * MLA_QK), MLA_Q_LORA ** -0.5),
        'w_mla_ukv': normal(ks[17], (DEPTH, MLA_KV_LORA, MLA_HEADS * (MLA_NOPE + MLA_V)), MLA_KV_LORA ** -0.5),
        'g_mla_q': gain(ks[18], (DEPTH, MLA_QK)),
        'g_mla_k': gain(ks[19], (DEPTH, MLA_QK)),
        'w_mla_o': normal(ks[20], (DEPTH, MLA_V_W, D), MLA_V_W ** -0.5),
        'w_fnet': normal(ks[21], (DEPTH, FNET_W, D), FNET_W ** -0.5),
        'w_out': normal(ks[22], (DEPTH, D, D), D ** -0.5),
        'ffn_w1': normal(ks[23], (N_DENSE_LAYERS, D, D_FF_DENSE), D ** -0.5),
        'ffn_w3': normal(ks[24], (N_DENSE_LAYERS, D, D_FF_DENSE), D ** -0.5),
        'ffn_w2': normal(ks[25], (N_DENSE_LAYERS, D_FF_DENSE, D), D_FF_DENSE ** -0.5),
        'moe_router': normal(ks[26], (N_MOE_LAYERS, D, N_EXPERTS), D ** -0.5),
        'moe_w1': normal(ks[27], (N_MOE_LAYERS, N_EXPERTS, D, D_FF_EXPERT), D ** -0.5),
        'moe_w3': normal(ks[28], (N_MOE_LAYERS, N_EXPERTS, D, D_FF_EXPERT), D ** -0.5),
        'moe_w2': normal(ks[29], (N_MOE_LAYERS, N_EXPERTS, D_FF_EXPERT, D), D_FF_EXPERT ** -0.5),
    }


def reference(x, c, ctx, c_ctx, w_mod, b_mod, g_norm1, g_norm2, w_in, g_diff_q, g_diff_k, diff_lambda,
              g_diff_out, w_diff_o, g_mla_cq, g_mla_ckv, w_mla_uq, w_mla_ukv, g_mla_q, g_mla_k, w_mla_o,
              w_fnet, w_out, ffn_w1, ffn_w3, ffn_w2, moe_router, moe_w1, moe_w3, moe_w2):
    n_lat = x.shape[1]
    rope_diff = axial_rope_tables(n_lat, DIFF_HEAD_DIM)
    rope_mla = axial_rope_tables(n_lat, MLA_ROPE)
    silu_c = jax.nn.silu(c)
    silu_cc = jax.nn.silu(c_ctx)
    xl, xc = x, ctx
    for i in range(DEPTH):
        need_ctx = i < DEPTH - 1
        mod_l = (silu_c @ w_mod[i] + b_mod[i])[:, None, :]
        mod_c = silu_cc @ w_mod[i] + b_mod[i]
        sh1_l, sc1_l, g1_l, sh2_l, sc2_l, g2_l = jnp.split(mod_l, N_MOD, axis=-1)
        sh1_c, sc1_c, g1_c, sh2_c, sc2_c, g2_c = jnp.split(mod_c, N_MOD, axis=-1)
        lp = {
            'w_in': w_in[i], 'g_diff_q': g_diff_q[i], 'g_diff_k': g_diff_k[i], 'diff_lambda': diff_lambda[i],
            'g_diff_out': g_diff_out[i], 'w_diff_o': w_diff_o[i], 'g_mla_cq': g_mla_cq[i],
            'g_mla_ckv': g_mla_ckv[i], 'w_mla_uq': w_mla_uq[i], 'w_mla_ukv': w_mla_ukv[i],
            'g_mla_q': g_mla_q[i], 'g_mla_k': g_mla_k[i], 'w_mla_o': w_mla_o[i], 'w_fnet': w_fnet[i],
            'w_out': w_out[i],
        }
        ul = modulate(xl, g_norm1[i], sh1_l, sc1_l)
        uc = modulate(xc, g_norm1[i], sh1_c, sc1_c)
        yc, yl = token_mixers(uc, ul, lp, i, need_ctx, rope_diff, rope_mla)
        if i % 2 == 0:
            ffn = functools.partial(dense_swiglu, w1=ffn_w1[i // 2], w3=ffn_w3[i // 2], w2=ffn_w2[i // 2])
        else:
            ffn = functools.partial(moe_swiglu, w_router=moe_router[i // 2], w1=moe_w1[i // 2],
                                    w3=moe_w3[i // 2], w2=moe_w2[i // 2])
        xl = xl + g1_l * yl
        xl = xl + g2_l * ffn(modulate(xl, g_norm2[i], sh2_l, sc2_l))
        if need_ctx:
            xc = xc + g1_c * yc
            xc = xc + g2_c * ffn(modulate(xc, g_norm2[i], sh2_c, sc2_c))
    return xl
```

```python
import functools
import math

import jax
import jax.numpy as jnp
from jax import lax
from jax.experimental import pallas as pl
from jax.experimental.pallas import tpu as pltpu

F32 = jnp.float32
BF16 = jnp.bfloat16

D = 2048
BATCH = 16
SEQ = 2048
DEPTH = 4
GRID_W = 64
CTX = 256
N_LAT = BATCH * SEQ
N_CTX = BATCH * CTX
N_TOK = N_LAT + N_CTX
HEADS = 8
FNET_W = 1024
FNET_GROUP_W = 256
MLA_Q_LORA = 512
MLA_KV_LORA = 256
MLA_NOPE = 128
MLA_ROPE = 64
MLA_QK = MLA_NOPE + MLA_ROPE
D_FF_DENSE = 5632
N_EXPERTS = 8
D_FF_EXPERT = 4096
ROPE_THETA = 10000.0
EPS = 1e-6
N_MOD = 6

Z_F, Z_DQ, Z_DK, Z_DV, Z_CQ, Z_CKV, Z_KR, Z_GATE = 0, 1024, 2048, 3072, 4096, 4608, 4864, 5120
Z_COLS = Z_GATE + 3 * D

LANE = 128
VMEM_LIMIT = 56 * 1024 * 1024

TM = 1024
TM_FFN2 = 512
TQ = 256
TM_MOE = 512
TF_MOE = 512
TG = 256


def _cp(n_axes):
    return pltpu.CompilerParams(dimension_semantics=("arbitrary",) * n_axes,
                                vmem_limit_bytes=VMEM_LIMIT)


def _dot(a, b):
    return jnp.dot(a, b, preferred_element_type=F32)


def _dot_nt(a, b):
    return lax.dot_general(a, b, (((1,), (1,)), ((), ())), preferred_element_type=F32)


def _sigmoid(x):
    return 1.0 / (1.0 + jnp.exp(-x))


def _mod_row(tm):
    return lambda i: jnp.minimum((i * tm) // SEQ, BATCH)


def _mod_kernel(c_ref, w_ref, b_ref, o_ref):
    c = c_ref[...]
    a = (c * _sigmoid(c)).astype(BF16)
    o_ref[0] = _dot(a, w_ref[0].astype(BF16)) + b_ref[0]


def mod_tables(c_all, w_mod, b_mod):
    tn = 1024
    rows = c_all.shape[0]
    return pl.pallas_call(
        _mod_kernel,
        grid=(DEPTH, N_MOD * D // tn),
        in_specs=[pl.BlockSpec((rows, D), lambda l, j: (0, 0)),
                  pl.BlockSpec((1, D, tn), lambda l, j: (l, 0, j)),
                  pl.BlockSpec((1, 1, tn), lambda l, j: (l, 0, j))],
        out_specs=pl.BlockSpec((1, rows, tn), lambda l, j: (l, 0, j)),
        out_shape=jax.ShapeDtypeStruct((DEPTH, rows, N_MOD * D), F32),
        compiler_params=_cp(2), name="mod_tables",
    )(c_all, w_mod, b_mod.reshape(DEPTH, 1, N_MOD * D))


def _modulate(x, g, shift, scale):
    ms = jnp.mean(x * x, axis=-1, keepdims=True)
    y = x * lax.rsqrt(ms + EPS) * g
    return y * (1.0 + scale) + shift


def _norm_mm_kernel(x_ref, g_ref, sh_ref, sc_ref, w_ref, o_ref, u_scr):
    @pl.when(pl.program_id(1) == 0)
    def _():
        u_scr[...] = _modulate(x_ref[...], g_ref[...], sh_ref[0], sc_ref[0]).astype(BF16)

    o_ref[...] = _dot(u_scr[...], w_ref[...]).astype(o_ref.dtype)


def _norm_swiglu_kernel(x_ref, g_ref, sh_ref, sc_ref, w1_ref, w3_ref, o_ref, u_scr):
    @pl.when(pl.program_id(1) == 0)
    def _():
        u_scr[...] = _modulate(x_ref[...], g_ref[...], sh_ref[0], sc_ref[0]).astype(BF16)

    u = u_scr[...]
    a = _dot(u, w1_ref[...])
    b = _dot(u, w3_ref[...])
    o_ref[...] = (a * _sigmoid(a) * b).astype(o_ref.dtype)


def norm_matmul(x, g, mod, chunk, ws, tn, n_rows):
    n = ws[0].shape[1]
    tm = TM
    kern = _norm_mm_kernel if len(ws) == 1 else _norm_swiglu_kernel
    row = _mod_row(tm)
    return pl.pallas_call(
        kern,
        grid=(n_rows // tm, n // tn),
        in_specs=[pl.BlockSpec((tm, D), lambda i, j: (i, 0)),
                  pl.BlockSpec((1, D), lambda i, j: (0, 0)),
                  pl.BlockSpec((1, 1, D), lambda i, j: (row(i), 0, chunk)),
                  pl.BlockSpec((1, 1, D), lambda i, j: (row(i), 0, chunk + 1))]
                 + [pl.BlockSpec((D, tn), lambda i, j: (0, j)) for _ in ws],
        out_specs=pl.BlockSpec((tm, tn), lambda i, j: (i, j)),
        out_shape=jax.ShapeDtypeStruct((n_rows, n), BF16),
        scratch_shapes=[pltpu.VMEM((tm, D), BF16)],
        compiler_params=_cp(2), name="norm_matmul%d" % len(ws),
    )(x, g, mod, mod, *ws)


def _mm_res_kernel(a_ref, w_ref, x_ref, gate_ref, o_ref):
    o_ref[...] = x_ref[...] + gate_ref[0] * _dot(a_ref[...], w_ref[...])


def matmul_residual(a, w, x, mod, chunk, tm, tn, n_rows):
    k = a.shape[1]
    row = _mod_row(tm)
    nj = D // tn
    return pl.pallas_call(
        _mm_res_kernel,
        grid=(n_rows // tm, nj),
        in_specs=[pl.BlockSpec((tm, k), lambda i, j: (i, 0)),
                  pl.BlockSpec((k, tn), lambda i, j: (0, j)),
                  pl.BlockSpec((tm, tn), lambda i, j: (i, j)),
                  pl.BlockSpec((1, 1, tn), lambda i, j: (row(i), 0, chunk * nj + j))],
        out_specs=pl.BlockSpec((tm, tn), lambda i, j: (i, j)),
        out_shape=jax.ShapeDtypeStruct((n_rows, D), F32),
        compiler_params=_cp(2), name="matmul_residual",
    )(a, w, x, mod)


def _dft_ch_kernel(z_ref, cw_ref, sw_ref, zc_ref, zs_ref):
    z = z_ref[...]
    zc_ref[...] = _dot(z, cw_ref[...]).astype(BF16)
    zs_ref[...] = _dot(z, sw_ref[...]).astype(BF16)


def dft_channels(z, cw, sw):
    tm = 1024
    gw = FNET_GROUP_W
    spec = pl.BlockSpec((tm, gw), lambda i, g: (i, g))
    wspec = pl.BlockSpec((gw, gw), lambda i, g: (0, 0))
    return pl.pallas_call(
        _dft_ch_kernel,
        grid=(N_TOK // tm, FNET_W // gw),
        in_specs=[spec, wspec, wspec],
        out_specs=[spec, spec],
        out_shape=[jax.ShapeDtypeStruct((N_TOK, FNET_W), BF16)] * 2,
        compiler_params=_cp(2), name="dft_channels",
    )(z, cw, sw)


def _dft_pos_kernel(cn_ref, sn_ref, zc_ref, zs_ref, y_ref):
    y_ref[...] = (_dot(cn_ref[...], zc_ref[...]) + _dot(sn_ref[...], zs_ref[...])).astype(BF16)


def dft_positions(cn, sn, zc, zs, n, row_blk0, tm):
    mt = n // tm
    zspec = pl.BlockSpec((n, FNET_W), lambda b, m: (row_blk0 + b, 0))
    mspec = pl.BlockSpec((tm, n), lambda b, m: (m, 0))
    return pl.pallas_call(
        _dft_pos_kernel,
        grid=(BATCH, mt),
        in_specs=[mspec, mspec, zspec, zspec],
        out_specs=pl.BlockSpec((tm, FNET_W), lambda b, m: (b * mt + m, 0)),
        out_shape=jax.ShapeDtypeStruct((BATCH * n, FNET_W), BF16),
        compiler_params=_cp(2), name="dft_positions",
    )(cn, sn, zc, zs)


def dft_matrices(n, scale):
    idx = jnp.arange(n, dtype=jnp.int32)
    prod = (idx[:, None] * idx[None, :]) % n
    ang = prod.astype(F32) * (2.0 * math.pi / n)
    return (jnp.cos(ang) * scale).astype(BF16), (jnp.sin(ang) * scale).astype(BF16)


def rope_tables(n_lat, n_ctx):
    n_freq = 16
    inv_freq = ROPE_THETA ** (-jnp.arange(n_freq, dtype=F32) / n_freq)
    cos_parts, sin_parts = [], []
    if n_lat:
        rows = n_lat // GRID_W
        row = jnp.repeat(jnp.arange(rows, dtype=F32), GRID_W)
        col = jnp.tile(jnp.arange(GRID_W, dtype=F32), rows)
        ang = jnp.concatenate([row[:, None] * inv_freq, col[:, None] * inv_freq], axis=-1)
        cos_parts.append(jnp.cos(ang))
        sin_parts.append(jnp.sin(ang))
    if n_ctx:
        cos_parts.append(jnp.ones((n_ctx, 32), F32))
        sin_parts.append(jnp.zeros((n_ctx, 32), F32))
    cos = jnp.concatenate(cos_parts, axis=0)
    sin = jnp.concatenate(sin_parts, axis=0)
    cos_f = jnp.tile(jnp.concatenate([cos, cos], axis=-1), (1, 2))
    sin_s = jnp.tile(jnp.concatenate([-sin, sin], axis=-1), (1, 2))
    return cos_f, sin_s


def _rope(x, cos_f, sin_s, first_half):
    swapped = jnp.where(first_half, pltpu.roll(x, 96, 1), pltpu.roll(x, 32, 1))
    return x * cos_f + swapped * sin_s


def _diff_attn_kernel(n_seg, seg_rows, *refs):
    lam_ref, q_ref = refs[0], refs[1]
    k_refs = refs[2:2 + n_seg]
    v_refs = refs[2 + n_seg:2 + 2 * n_seg]
    gq_ref, gk_ref, go_ref, cq_ref, sq_ref, ck_ref, sk_ref, o_ref, k_scr, v_scr = refs[2 + 2 * n_seg:]

    lane = lax.broadcasted_iota(jnp.int32, (1, LANE), 1)
    first_half = (lane % 64) < 32
    comp0 = lane < 64

    def head_norm(x, g):
        sq = x * x
        s0 = jnp.sum(jnp.where(comp0, sq, 0.0), axis=-1, keepdims=True)
        s1 = jnp.sum(sq, axis=-1, keepdims=True) - s0
        ms = jnp.where(comp0, s0, s1) * (1.0 / 64.0)
        return x * lax.rsqrt(ms + EPS) * g

    @pl.when(pl.program_id(2) == 0)
    def _():
        r0 = 0
        for s in range(n_seg):
            n = seg_rows[s]
            k = head_norm(k_refs[s][...].astype(F32), gk_ref[...])
            k = _rope(k, ck_ref[r0:r0 + n, :], sk_ref[r0:r0 + n, :], first_half)
            k_scr[r0:r0 + n, :] = k.astype(BF16)
            v_scr[r0:r0 + n, :] = v_refs[s][...]
            r0 += n

    q = head_norm(q_ref[...].astype(F32), gq_ref[...])
    q = _rope(q, cq_ref[...], sq_ref[...], first_half) * (64.0 ** -0.5)
    k = k_scr[...]
    s0 = _dot_nt(jnp.where(comp0, q, 0.0).astype(BF16), k)
    s1 = _dot_nt(jnp.where(comp0, 0.0, q).astype(BF16), k)
    e0 = jnp.exp(s0 - jnp.max(s0, axis=-1, keepdims=True))
    e1 = jnp.exp(s1 - jnp.max(s1, axis=-1, keepdims=True))
    r0_ = 1.0 / jnp.sum(e0, axis=-1, keepdims=True)
    r1_ = lam_ref[0] / jnp.sum(e1, axis=-1, keepdims=True)
    w = (e0 * r0_ - e1 * r1_).astype(BF16)
    o = _dot(w, v_scr[...])
    ms = jnp.mean(o * o, axis=-1, keepdims=True)
    o_ref[...] = (o * lax.rsqrt(ms + EPS) * go_ref[...]).astype(BF16)


def diff_attention(z, lam, gq, gk, go, cos_f, sin_s, n_q, q_blk0, segs, tq):
    n_seg = len(segs)
    seg_rows = tuple(r for r, _ in segs)
    n_k = sum(seg_rows)
    nqb = n_q // tq
    hq, hk, hv = Z_DQ // LANE, Z_DK // LANE, Z_DV // LANE

    def kspec(rows, off, col0):
        return pl.BlockSpec((rows, LANE), lambda b, h, qb: (off + b, col0 + h))

    vec = pl.BlockSpec((1, LANE), lambda b, h, qb: (0, 0))
    in_specs = ([pl.BlockSpec(memory_space=pltpu.SMEM),
                 pl.BlockSpec((tq, LANE), lambda b, h, qb: ((q_blk0 + b) * nqb + qb, hq + h))]
                + [kspec(r, off, hk) for r, off in segs]
                + [kspec(r, off, hv) for r, off in segs]
                + [vec, vec, vec,
                   pl.BlockSpec((tq, LANE), lambda b, h, qb: (qb, 0)),
                   pl.BlockSpec((tq, LANE), lambda b, h, qb: (qb, 0)),
                   pl.BlockSpec((n_k, LANE), lambda b, h, qb: (0, 0)),
                   pl.BlockSpec((n_k, LANE), lambda b, h, qb: (0, 0))])
    return pl.pallas_call(
        functools.partial(_diff_attn_kernel, n_seg, seg_rows),
        grid=(BATCH, HEADS, nqb),
        in_specs=in_specs,
        out_specs=pl.BlockSpec((tq, LANE), lambda b, h, qb: (b * nqb + qb, h)),
        out_shape=jax.ShapeDtypeStruct((BATCH * n_q, HEADS * LANE), BF16),
        scratch_shapes=[pltpu.VMEM((n_k, LANE), BF16), pltpu.VMEM((n_k, LANE), BF16)],
        compiler_params=_cp(3), name="diff_attention",
    )(lam, z, *([z] * (2 * n_seg)), gq, gk, go, cos_f, sin_s, cos_f, sin_s)


def _mla_up_kernel(cq_ref, ckv_ref, gq_ref, gkv_ref, wq_ref, wkv_ref, q_ref, kv_ref):
    def norm(x, g):
        ms = jnp.mean(x * x, axis=-1, keepdims=True)
        return (x * lax.rsqrt(ms + EPS) * g).astype(BF16)

    q_ref[...] = _dot(norm(cq_ref[...].astype(F32), gq_ref[...]), wq_ref[...]).astype(BF16)
    kv_ref[...] = _dot(norm(ckv_ref[...].astype(F32), gkv_ref[...]), wkv_ref[...]).astype(BF16)


def mla_up(z, g_cq, g_ckv, w_uq, w_ukv):
    tm = 1024
    n = HEADS * 256
    return pl.pallas_call(
        _mla_up_kernel,
        grid=(N_TOK // tm,),
        in_specs=[pl.BlockSpec((tm, MLA_Q_LORA), lambda i: (i, Z_CQ // MLA_Q_LORA)),
                  pl.BlockSpec((tm, MLA_KV_LORA), lambda i: (i, Z_CKV // MLA_KV_LORA)),
                  pl.BlockSpec((1, MLA_Q_LORA), lambda i: (0, 0)),
                  pl.BlockSpec((1, MLA_KV_LORA), lambda i: (0, 0)),
                  pl.BlockSpec((MLA_Q_LORA, n), lambda i: (0, 0)),
                  pl.BlockSpec((MLA_KV_LORA, n), lambda i: (0, 0))],
        out_specs=[pl.BlockSpec((tm, n), lambda i: (i, 0))] * 2,
        out_shape=[jax.ShapeDtypeStruct((N_TOK, n), BF16)] * 2,
        compiler_params=_cp(1), name="mla_up",
    )(z, z, g_cq, g_ckv, w_uq, w_ukv)


def _mla_attn_kernel(n_seg, seg_rows, *refs):
    q_ref = refs[0]
    kv_refs = refs[1:1 + n_seg]
    kr_refs = refs[1 + n_seg:1 + 2 * n_seg]
    gq_ref, gk_ref, cq_ref, sq_ref, ck_ref, sk_ref, o_ref, k_scr, v_scr = refs[1 + 2 * n_seg:]

    lane = lax.broadcasted_iota(jnp.int32, (1, LANE), 1)
    first_half = (lane % 64) < 32

    def qk_norm(nope, rope, g_ref, scale):
        ss = jnp.sum(nope * nope, axis=-1, keepdims=True) + jnp.sum(rope * rope, axis=-1, keepdims=True)
        inv = lax.rsqrt(ss * (1.0 / MLA_QK) + EPS) * scale
        return ((nope * inv * g_ref[:, :LANE]).astype(BF16), (rope * inv * g_ref[:, LANE:]).astype(BF16))

    @pl.when(pl.program_id(2) == 0)
    def _():
        r0 = 0
        for s in range(n_seg):
            n = seg_rows[s]
            kv = kv_refs[s][...]
            rope = _rope(kr_refs[s][...].astype(F32), ck_ref[r0:r0 + n, :], sk_ref[r0:r0 + n, :], first_half)
            kn, kr = qk_norm(kv[:, :LANE].astype(F32), rope, gk_ref, 1.0)
            k_scr[r0:r0 + n, :LANE] = kn
            k_scr[r0:r0 + n, LANE:] = kr
            v_scr[r0:r0 + n, :] = kv[:, LANE:]
            r0 += n

    q = q_ref[...]
    rope = _rope(q[:, LANE:].astype(F32), cq_ref[...], sq_ref[...], first_half)
    qn, qr = qk_norm(q[:, :LANE].astype(F32), rope, gq_ref, MLA_QK ** -0.5)
    s = _dot_nt(jnp.concatenate([qn, qr], axis=1), k_scr[...])
    e = jnp.exp(s - jnp.max(s, axis=-1, keepdims=True))
    r = 1.0 / jnp.sum(e, axis=-1, keepdims=True)
    o_ref[...] = (_dot(e.astype(BF16), v_scr[...]) * r).astype(BF16)


def mla_attention(q_mla, kv, z, gq, gk, cos_f, sin_s, n_q, q_blk0, segs, tq):
    n_seg = len(segs)
    seg_rows = tuple(r for r, _ in segs)
    n_k = sum(seg_rows)
    nqb = n_q // tq
    ckr = Z_KR // LANE
    vec = pl.BlockSpec((1, 2 * LANE), lambda b, h, qb: (0, 0))
    in_specs = ([pl.BlockSpec((tq, 2 * LANE), lambda b, h, qb: ((q_blk0 + b) * nqb + qb, h))]
                + [pl.BlockSpec((r, 2 * LANE), (lambda off: lambda b, h, qb: (off + b, h))(off)) for r, off in segs]
                + [pl.BlockSpec((r, LANE), (lambda off: lambda b, h, qb: (off + b, ckr))(off)) for r, off in segs]
                + [vec, vec,
                   pl.BlockSpec((tq, LANE), lambda b, h, qb: (qb, 0)),
                   pl.BlockSpec((tq, LANE), lambda b, h, qb: (qb, 0)),
                   pl.BlockSpec((n_k, LANE), lambda b, h, qb: (0, 0)),
                   pl.BlockSpec((n_k, LANE), lambda b, h, qb: (0, 0))])
    return pl.pallas_call(
        functools.partial(_mla_attn_kernel, n_seg, seg_rows),
        grid=(BATCH, HEADS, nqb),
        in_specs=in_specs,
        out_specs=pl.BlockSpec((tq, LANE), lambda b, h, qb: (b * nqb + qb, h)),
        out_shape=jax.ShapeDtypeStruct((BATCH * n_q, HEADS * LANE), BF16),
        scratch_shapes=[pltpu.VMEM((n_k, 2 * LANE), BF16), pltpu.VMEM((n_k, LANE), BF16)],
        compiler_params=_cp(3), name="mla_attention",
    )(q_mla, *([kv] * n_seg), *([z] * n_seg), gq, gk, cos_f, sin_s, cos_f, sin_s)


def _merge_kernel(yf_ref, od_ref, om_ref, g0_ref, g1_ref, g2_ref, wf_ref, wd_ref, wm_ref, o_ref):
    f = _dot(yf_ref[...], wf_ref[...])
    d = _dot(od_ref[...], wd_ref[...])
    m = _dot(om_ref[...], wm_ref[...])
    o = (_sigmoid(g0_ref[...].astype(F32)) * f + _sigmoid(g1_ref[...].astype(F32)) * d
         + _sigmoid(g2_ref[...].astype(F32)) * m)
    o_ref[...] = o.astype(BF16)


def merge_branches(yf, od, om, z, w_f, w_d, w_m, n_rows):
    tm, tn = TM, 512
    k = FNET_W
    a_spec = pl.BlockSpec((tm, k), lambda i, j: (i, 0))
    w_spec = pl.BlockSpec((k, tn), lambda i, j: (0, j))
    g0 = Z_GATE // tn

    def gspec(r):
        return pl.BlockSpec((tm, tn), lambda i, j: (i, g0 + r * (D // tn) + j))

    return pl.pallas_call(
        _merge_kernel,
        grid=(n_rows // tm, D // tn),
        in_specs=[a_spec, a_spec, a_spec, gspec(0), gspec(1), gspec(2), w_spec, w_spec, w_spec],
        out_specs=pl.BlockSpec((tm, tn), lambda i, j: (i, j)),
        out_shape=jax.ShapeDtypeStruct((n_rows, D), BF16),
        compiler_params=_cp(2), name="merge_branches",
    )(yf, od, om, z, z, z, w_f, w_d, w_m)


def _split3(x):
    hi = x.astype(BF16)
    r = x - hi.astype(F32)
    mid = r.astype(BF16)
    lo = (r - mid.astype(F32)).astype(BF16)
    return hi, mid, lo


def _router_kernel(x_ref, g_ref, sh_ref, sc_ref, wh_ref, wm_ref, wl_ref, u_ref, rt_ref):
    u = _modulate(x_ref[...], g_ref[...], sh_ref[0], sc_ref[0])
    u_ref[...] = u
    uh, um, ul = _split3(u)
    wh, wm, wl = wh_ref[...], wm_ref[...], wl_ref[...]
    logits = (_dot(ul, wh) + _dot(uh, wl) + _dot(um, wm)) + (_dot(um, wh) + _dot(uh, wm)) + _dot(uh, wh)
    lane = lax.broadcasted_iota(jnp.int32, logits.shape, 1).astype(F32)
    neg = -jnp.inf
    lg = jnp.where(lane < N_EXPERTS, logits, neg)
    m1 = jnp.max(lg, axis=-1, keepdims=True)
    i1 = jnp.min(jnp.where(lg == m1, lane, float(LANE)), axis=-1, keepdims=True)
    lg2 = jnp.where(lane == i1, neg, lg)
    m2 = jnp.max(lg2, axis=-1, keepdims=True)
    i2 = jnp.min(jnp.where(lg2 == m2, lane, float(LANE)), axis=-1, keepdims=True)
    p1 = 1.0 / (1.0 + jnp.exp(m2 - m1))
    p2 = 1.0 - p1
    rt = jnp.where(lane == 0.0, i1, jnp.where(lane == 1.0, i2, jnp.where(lane == 2.0, p1,
                                                                      jnp.where(lane == 3.0, p2, 0.0))))
    rt_ref[...] = rt


def moe_router(x, g, mod, chunk, w_router, n_rows):
    tm = 512
    row = _mod_row(tm)
    wpad = jnp.pad(w_router, ((0, 0), (0, LANE - N_EXPERTS)))
    wh, wm, wl = _split3(wpad)
    wspec = pl.BlockSpec((D, LANE), lambda i: (0, 0))
    return pl.pallas_call(
        _router_kernel,
        grid=(n_rows // tm,),
        in_specs=[pl.BlockSpec((tm, D), lambda i: (i, 0)),
                  pl.BlockSpec((1, D), lambda i: (0, 0)),
                  pl.BlockSpec((1, 1, D), lambda i: (row(i), 0, chunk)),
                  pl.BlockSpec((1, 1, D), lambda i: (row(i), 0, chunk + 1)),
                  wspec, wspec, wspec],
        out_specs=[pl.BlockSpec((tm, D), lambda i: (i, 0)), pl.BlockSpec((tm, LANE), lambda i: (i, 0))],
        out_shape=[jax.ShapeDtypeStruct((n_rows, D), F32), jax.ShapeDtypeStruct((n_rows, LANE), F32)],
        compiler_params=_cp(1), name="moe_router",
    )(x, g, mod, mod, wh, wm, wl)


def _row_copy(src_hbm, dst_ref, sem, src_row, dst_row):
    return pltpu.make_async_copy(src_hbm.at[pl.ds(src_row, 1), :], dst_ref.at[pl.ds(dst_row, 1), :], sem)


def _gather_kernel(idx_ref, src_hbm, o_ref, sem):
    base = pl.program_id(0) * TG

    def issue(r, c):
        _row_copy(src_hbm, o_ref, sem, idx_ref[base + r], r).start()
        return c

    lax.fori_loop(0, TG, issue, 0)

    def wait(r, c):
        _row_copy(src_hbm, o_ref, sem, 0, r).wait()
        return c

    lax.fori_loop(0, TG, wait, 0)


def gather_rows(src, idx):
    n_out = idx.shape[0]
    return pl.pallas_call(
        _gather_kernel,
        grid_spec=pltpu.PrefetchScalarGridSpec(
            num_scalar_prefetch=1, grid=(n_out // TG,),
            in_specs=[pl.BlockSpec(memory_space=pl.ANY)],
            out_specs=pl.BlockSpec((TG, D), lambda i, idx: (i, 0)),
            scratch_shapes=[pltpu.SemaphoreType.DMA]),
        out_shape=jax.ShapeDtypeStruct((n_out, D), src.dtype),
        compiler_params=_cp(1), name="gather_rows",
    )(idx, src)


def _moe_kernel(te_ref, nu_ref, xs_ref, w1_ref, w3_ref, w2_ref, o_ref, xb_scr, acc_scr):
    i, j = pl.program_id(0), pl.program_id(1)
    last = pl.num_programs(1) - 1
    used = i < nu_ref[0]

    @pl.when(used)
    def _():
        @pl.when(j == 0)
        def _():
            xb_scr[...] = xs_ref[...].astype(BF16)
            acc_scr[...] = jnp.zeros_like(acc_scr)

        xb = xb_scr[...]
        a = _dot(xb, w1_ref[0])
        b = _dot(xb, w3_ref[0])
        h = (a * _sigmoid(a) * b).astype(BF16)
        acc_scr[...] += _dot(h, w2_ref[0])

        @pl.when(j == last)
        def _():
            o_ref[...] = acc_scr[...]

    @pl.when(jnp.logical_and(jnp.logical_not(used), j == last))
    def _():
        o_ref[...] = jnp.zeros_like(o_ref)


def moe_experts(xs, tile_expert, n_used, w1, w3, w2):
    p = xs.shape[0]
    tm, tf = TM_MOE, TF_MOE
    return pl.pallas_call(
        _moe_kernel,
        grid_spec=pltpu.PrefetchScalarGridSpec(
            num_scalar_prefetch=2, grid=(p // tm, D_FF_EXPERT // tf),
            in_specs=[pl.BlockSpec((tm, D), lambda i, j, te, nu: (i, 0)),
                      pl.BlockSpec((1, D, tf), lambda i, j, te, nu: (te[i], 0, j)),
                      pl.BlockSpec((1, D, tf), lambda i, j, te, nu: (te[i], 0, j)),
                      pl.BlockSpec((1, tf, D), lambda i, j, te, nu: (te[i], j, 0))],
            out_specs=pl.BlockSpec((tm, D), lambda i, j, te, nu: (i, 0)),
            scratch_shapes=[pltpu.VMEM((tm, D), BF16), pltpu.VMEM((tm, D), F32)]),
        out_shape=jax.ShapeDtypeStruct((p, D), F32),
        compiler_params=_cp(2), name="moe_experts",
    )(tile_expert, n_used, xs, w1, w3, w2)


def _combine_kernel(n_tok, pos_ref, ys_hbm, x_ref, rt_ref, gate_ref, o_ref, buf, sem):
    base = pl.program_id(0) * TG

    def issue(r, c):
        _row_copy(ys_hbm, buf.at[0], sem, pos_ref[base + r], r).start()
        _row_copy(ys_hbm, buf.at[1], sem, pos_ref[n_tok + base + r], r).start()
        return c

    lax.fori_loop(0, TG, issue, 0)

    def wait(r, c):
        _row_copy(ys_hbm, buf.at[0], sem, 0, r).wait()
        _row_copy(ys_hbm, buf.at[1], sem, 0, r).wait()
        return c

    lax.fori_loop(0, TG, wait, 0)
    rt = rt_ref[...]
    y = rt[:, 2:3] * buf[0] + rt[:, 3:4] * buf[1]
    o_ref[...] = x_ref[...] + gate_ref[0] * y


def moe_combine(ys, pos, x, rt, mod, chunk, n_rows):
    row = _mod_row(TG)
    return pl.pallas_call(
        functools.partial(_combine_kernel, n_rows),
        grid_spec=pltpu.PrefetchScalarGridSpec(
            num_scalar_prefetch=1, grid=(n_rows // TG,),
            in_specs=[pl.BlockSpec(memory_space=pl.ANY),
                      pl.BlockSpec((TG, D), lambda i, pos: (i, 0)),
                      pl.BlockSpec((TG, LANE), lambda i, pos: (i, 0)),
                      pl.BlockSpec((1, 1, D), lambda i, pos: (row(i), 0, chunk))],
            out_specs=pl.BlockSpec((TG, D), lambda i, pos: (i, 0)),
            scratch_shapes=[pltpu.VMEM((2, TG, D), F32), pltpu.SemaphoreType.DMA]),
        out_shape=jax.ShapeDtypeStruct((n_rows, D), F32),
        compiler_params=_cp(1), name="moe_combine",
    )(pos, ys, x, rt, mod)


def moe_ffn(x, g, mod, w_router, w1, w3, w2, n_rows):
    u, rt = moe_router(x, g, mod, 3, w_router, n_rows)
    tm = TM_MOE
    n_pairs = 2 * n_rows
    p_rows = n_pairs + N_EXPERTS * tm
    n_tiles = p_rows // tm
    e_flat = jnp.concatenate([rt[:, 0], rt[:, 1]]).astype(jnp.int32)
    onehot = (e_flat[:, None] == jnp.arange(N_EXPERTS, dtype=jnp.int32)[None, :]).astype(jnp.int32)
    csum = jnp.cumsum(onehot, axis=0)
    counts = csum[-1]
    rank = jnp.sum((csum - onehot) * onehot, axis=1)
    padded = ((counts + tm - 1) // tm) * tm
    g_end = jnp.cumsum(padded)
    g_start = g_end - padded
    dest = g_start[e_flat] + rank
    tok = jnp.tile(jnp.arange(n_rows, dtype=jnp.int32), 2)
    src = jnp.zeros((p_rows,), jnp.int32).at[dest].set(tok)
    tile_start = jnp.arange(n_tiles, dtype=jnp.int32) * tm
    tile_expert = jnp.minimum(jnp.sum((tile_start[:, None] >= g_end[None, :]).astype(jnp.int32), axis=1),
                              N_EXPERTS - 1).astype(jnp.int32)
    n_used = (g_end[-1] // tm).astype(jnp.int32).reshape(1)
    xs = gather_rows(u, src)
    ys = moe_experts(xs, tile_expert, n_used, w1, w3, w2)
    return moe_combine(ys, dest, x, rt, mod, 5, n_rows)


def kernel(x, c, ctx, c_ctx, w_mod, b_mod, g_norm1, g_norm2, w_in, g_diff_q, g_diff_k, diff_lambda,
           g_diff_out, w_diff_o, g_mla_cq, g_mla_ckv, w_mla_uq, w_mla_ukv, g_mla_q, g_mla_k, w_mla_o,
           w_fnet, w_out, ffn_w1, ffn_w3, ffn_w2, moe_router, moe_w1, moe_w3, moe_w2):
    xs = jnp.concatenate([x.reshape(N_LAT, D), ctx.reshape(N_CTX, D)], axis=0)
    c_all = jnp.zeros((32, D), F32).at[:BATCH].set(c).at[BATCH].set(c_ctx)
    mods = mod_tables(c_all, w_mod, b_mod)

    cos_lat, sin_lat = rope_tables(SEQ, CTX)
    cos_ctx, sin_ctx = rope_tables(0, CTX)
    cw, sw = dft_matrices(FNET_GROUP_W, FNET_GROUP_W ** -0.5)
    cn_lat, sn_lat = dft_matrices(SEQ, SEQ ** -0.5)
    cn_ctx, sn_ctx = dft_matrices(CTX, CTX ** -0.5)
    sn_lat, sn_ctx = -sn_lat, -sn_ctx

    lat_segs = [(SEQ, 0), (CTX, N_LAT // CTX)]
    ctx_segs = [(CTX, N_LAT // CTX)]

    for i in range(DEPTH):
        need_ctx = i < DEPTH - 1
        n_rows = N_TOK if need_ctx else N_LAT
        mod = mods[i].reshape(32, 1, N_MOD * D)
        wi = w_in[i]
        w_in_p = jnp.concatenate(
            [wi[:, :Z_KR], wi[:, Z_KR:Z_KR + MLA_ROPE], jnp.zeros((D, Z_GATE - Z_KR - MLA_ROPE), F32),
             wi[:, Z_KR + MLA_ROPE:]], axis=1).astype(BF16)
        w_uq = jnp.pad(w_mla_uq[i].reshape(MLA_Q_LORA, HEADS, MLA_QK),
                       ((0, 0), (0, 0), (0, 256 - MLA_QK))).reshape(MLA_Q_LORA, HEADS * 256).astype(BF16)
        w_ukv = w_mla_ukv[i].astype(BF16)
        g_q = jnp.pad(g_mla_q[i], (0, 256 - MLA_QK)).reshape(1, 256)
        g_k = jnp.pad(g_mla_k[i], (0, 256 - MLA_QK)).reshape(1, 256)
        lam_init = 0.8 - 0.6 * math.exp(-0.3 * i)
        lp = diff_lambda[i]
        lam = (jnp.exp(jnp.sum(lp[0] * lp[1])) - jnp.exp(jnp.sum(lp[2] * lp[3])) + lam_init).reshape(1)
        gdq = jnp.tile(g_diff_q[i], 2).reshape(1, LANE)
        gdk = jnp.tile(g_diff_k[i], 2).reshape(1, LANE)
        gdo = (g_diff_out[i] * (1.0 - lam_init)).reshape(1, LANE)

        z = norm_matmul(xs, g_norm1[i].reshape(1, D), mod, 0, [w_in_p], 1024, N_TOK)
        zc, zs = dft_channels(z, cw, sw)
        yf = dft_positions(cn_lat, sn_lat, zc, zs, SEQ, 0, 512)
        q_mla, kv = mla_up(z, g_mla_cq[i].reshape(1, -1), g_mla_ckv[i].reshape(1, -1), w_uq, w_ukv)
        od = diff_attention(z, lam, gdq, gdk, gdo, cos_lat, sin_lat, SEQ, 0, lat_segs, TQ)
        om = mla_attention(q_mla, kv, z, g_q, g_k, cos_lat, sin_lat, SEQ, 0, lat_segs, TQ)
        if need_ctx:
            yf = jnp.concatenate([yf, dft_positions(cn_ctx, sn_ctx, zc, zs, CTX, N_LAT // CTX, CTX)], axis=0)
            od = jnp.concatenate(
                [od, diff_attention(z, lam, gdq, gdk, gdo, cos_ctx, sin_ctx, CTX, N_LAT // CTX, ctx_segs, CTX)],
                axis=0)
            om = jnp.concatenate(
                [om, mla_attention(q_mla, kv, z, g_q, g_k, cos_ctx, sin_ctx, CTX, N_LAT // CTX, ctx_segs, CTX)],
                axis=0)
        merged = merge_branches(yf, od, om, z, w_fnet[i].astype(BF16), w_diff_o[i].astype(BF16),
                                w_mla_o[i].astype(BF16), n_rows)
        xs = matmul_residual(merged, w_out[i].astype(BF16), xs, mod, 2, TM, 1024, n_rows)

        g2 = g_norm2[i].reshape(1, D)
        if i % 2 == 0:
            j = i // 2
            h = norm_matmul(xs, g2, mod, 3, [ffn_w1[j].astype(BF16), ffn_w3[j].astype(BF16)], 512, n_rows)
            xs = matmul_residual(h, ffn_w2[j].astype(BF16), xs, mod, 5, TM_FFN2, 512, n_rows)
        else:
            j = i // 2
            xs = moe_ffn(xs, g2, mod, moe_router[j], moe_w1[j].astype(BF16), moe_w3[j].astype(BF16),
                         moe_w2[j].astype(BF16), n_rows)
    return xs[:N_LAT].reshape(BATCH, SEQ, D)
```

```python
import functools
import math

import jax
import jax.numpy as jnp
from jax import lax
from jax.experimental import pallas as pl
from jax.experimental.pallas import tpu as pltpu

F32 = jnp.float32
BF16 = jnp.bfloat16

D = 2048
BATCH = 16
SEQ = 2048
DEPTH = 4
GRID_W = 64
CTX = 256
N_LAT = BATCH * SEQ
N_CTX = BATCH * CTX
N_TOK = N_LAT + N_CTX
HEADS = 8
FNET_W = 1024
FNET_GROUP_W = 256
MLA_Q_LORA = 512
MLA_KV_LORA = 256
MLA_NOPE = 128
MLA_ROPE = 64
MLA_QK = MLA_NOPE + MLA_ROPE
D_FF_DENSE = 5632
N_EXPERTS = 8
D_FF_EXPERT = 4096
ROPE_THETA = 10000.0
EPS = 1e-6
N_MOD = 6
LOG2E = math.log2(math.e)

Z_F, Z_DQ, Z_DK, Z_DV, Z_CQ, Z_CKV, Z_KR, Z_GATE = 0, 1024, 2048, 3072, 4096, 4608, 4864, 5120
Z_COLS = Z_GATE + 3 * D

LANE = 128
VMEM_LIMIT = 56 * 1024 * 1024

TM = 1024
TM_FFN2 = 512
TQ = 512
TQ_SUB = 256
TM_MOE = 512
TF_MOE = 512
TG = 256


def _cp(n_axes):
    return pltpu.CompilerParams(dimension_semantics=("arbitrary",) * n_axes,
                                vmem_limit_bytes=VMEM_LIMIT)


def _dot(a, b):
    return jnp.dot(a, b, preferred_element_type=F32)


def _dot_nt(a, b):
    return lax.dot_general(a, b, (((1,), (1,)), ((), ())), preferred_element_type=F32)


def _sigmoid(x):
    return 1.0 / (1.0 + jnp.exp(-x))


def _mod_row(tm):
    return lambda i: jnp.minimum((i * tm) // SEQ, BATCH)


def _mod_kernel(c_ref, w_ref, b_ref, o_ref):
    c = c_ref[...]
    a = (c * _sigmoid(c)).astype(BF16)
    o_ref[0] = _dot(a, w_ref[0].astype(BF16)) + b_ref[0]


def mod_tables(c_all, w_mod, b_mod):
    tn = 1024
    rows = c_all.shape[0]
    return pl.pallas_call(
        _mod_kernel,
        grid=(DEPTH, N_MOD * D // tn),
        in_specs=[pl.BlockSpec((rows, D), lambda l, j: (0, 0)),
                  pl.BlockSpec((1, D, tn), lambda l, j: (l, 0, j)),
                  pl.BlockSpec((1, 1, tn), lambda l, j: (l, 0, j))],
        out_specs=pl.BlockSpec((1, rows, tn), lambda l, j: (l, 0, j)),
        out_shape=jax.ShapeDtypeStruct((DEPTH, rows, N_MOD * D), F32),
        compiler_params=_cp(2), name="mod_tables",
    )(c_all, w_mod, b_mod.reshape(DEPTH, 1, N_MOD * D))


def _modulate(x, g, shift, scale):
    ms = jnp.mean(x * x, axis=-1, keepdims=True)
    y = x * lax.rsqrt(ms + EPS) * g
    return y * (1.0 + scale) + shift


def _norm_mm_kernel(x_ref, g_ref, sh_ref, sc_ref, w_ref, o_ref, u_scr):
    @pl.when(pl.program_id(1) == 0)
    def _():
        u_scr[...] = _modulate(x_ref[...], g_ref[...], sh_ref[0], sc_ref[0]).astype(BF16)

    o_ref[...] = _dot(u_scr[...], w_ref[...]).astype(o_ref.dtype)


def _norm_swiglu_kernel(x_ref, g_ref, sh_ref, sc_ref, w1_ref, w3_ref, o_ref, u_scr):
    @pl.when(pl.program_id(1) == 0)
    def _():
        u_scr[...] = _modulate(x_ref[...], g_ref[...], sh_ref[0], sc_ref[0]).astype(BF16)

    u = u_scr[...]
    a = _dot(u, w1_ref[...])
    b = _dot(u, w3_ref[...])
    o_ref[...] = (a * _sigmoid(a) * b).astype(o_ref.dtype)


def norm_matmul(x, g, mod, chunk, ws, tn, n_rows):
    n = ws[0].shape[1]
    tm = TM
    kern = _norm_mm_kernel if len(ws) == 1 else _norm_swiglu_kernel
    row = _mod_row(tm)
    return pl.pallas_call(
        kern,
        grid=(n_rows // tm, n // tn),
        in_specs=[pl.BlockSpec((tm, D), lambda i, j: (i, 0)),
                  pl.BlockSpec((1, D), lambda i, j: (0, 0)),
                  pl.BlockSpec((1, 1, D), lambda i, j: (row(i), 0, chunk)),
                  pl.BlockSpec((1, 1, D), lambda i, j: (row(i), 0, chunk + 1))]
                 + [pl.BlockSpec((D, tn), lambda i, j: (0, j)) for _ in ws],
        out_specs=pl.BlockSpec((tm, tn), lambda i, j: (i, j)),
        out_shape=jax.ShapeDtypeStruct((n_rows, n), BF16),
        scratch_shapes=[pltpu.VMEM((tm, D), BF16)],
        compiler_params=_cp(2), name="norm_matmul%d" % len(ws),
    )(x, g, mod, mod, *ws)


def _mm_res_kernel(a_ref, w_ref, x_ref, gate_ref, o_ref):
    o_ref[...] = x_ref[...] + gate_ref[0] * _dot(a_ref[...], w_ref[...])


def matmul_residual(a, w, x, mod, chunk, tm, tn, n_rows):
    k = a.shape[1]
    row = _mod_row(tm)
    nj = D // tn
    return pl.pallas_call(
        _mm_res_kernel,
        grid=(n_rows // tm, nj),
        in_specs=[pl.BlockSpec((tm, k), lambda i, j: (i, 0)),
                  pl.BlockSpec((k, tn), lambda i, j: (0, j)),
                  pl.BlockSpec((tm, tn), lambda i, j: (i, j)),
                  pl.BlockSpec((1, 1, tn), lambda i, j: (row(i), 0, chunk * nj + j))],
        out_specs=pl.BlockSpec((tm, tn), lambda i, j: (i, j)),
        out_shape=jax.ShapeDtypeStruct((n_rows, D), F32),
        compiler_params=_cp(2), name="matmul_residual",
    )(a, w, x, mod)


def _dft_ch_kernel(z_ref, cw_ref, sw_ref, zc_ref, zs_ref):
    z = z_ref[...]
    zc_ref[...] = _dot(z, cw_ref[...]).astype(BF16)
    zs_ref[...] = _dot(z, sw_ref[...]).astype(BF16)


def dft_channels(z, cw, sw):
    tm = 1024
    gw = FNET_GROUP_W
    spec = pl.BlockSpec((tm, gw), lambda i, g: (i, g))
    wspec = pl.BlockSpec((gw, gw), lambda i, g: (0, 0))
    return pl.pallas_call(
        _dft_ch_kernel,
        grid=(N_TOK // tm, FNET_W // gw),
        in_specs=[spec, wspec, wspec],
        out_specs=[spec, spec],
        out_shape=[jax.ShapeDtypeStruct((N_TOK, FNET_W), BF16)] * 2,
        compiler_params=_cp(2), name="dft_channels",
    )(z, cw, sw)


def _dft_pos_kernel(cn_ref, sn_ref, zc_ref, zs_ref, y_ref):
    y_ref[...] = (_dot(cn_ref[...], zc_ref[...]) + _dot(sn_ref[...], zs_ref[...])).astype(BF16)


def dft_positions(cn, sn, zc, zs, n, row_blk0, tm):
    mt = n // tm
    zspec = pl.BlockSpec((n, FNET_W), lambda b, m: (row_blk0 + b, 0))
    mspec = pl.BlockSpec((tm, n), lambda b, m: (m, 0))
    return pl.pallas_call(
        _dft_pos_kernel,
        grid=(BATCH, mt),
        in_specs=[mspec, mspec, zspec, zspec],
        out_specs=pl.BlockSpec((tm, FNET_W), lambda b, m: (b * mt + m, 0)),
        out_shape=jax.ShapeDtypeStruct((BATCH * n, FNET_W), BF16),
        compiler_params=_cp(2), name="dft_positions",
    )(cn, sn, zc, zs)


def dft_matrices(n, scale):
    idx = jnp.arange(n, dtype=jnp.int32)
    prod = (idx[:, None] * idx[None, :]) % n
    ang = prod.astype(F32) * (2.0 * math.pi / n)
    return (jnp.cos(ang) * scale).astype(BF16), (jnp.sin(ang) * scale).astype(BF16)


def rope_tables(n_lat, n_ctx):
    n_freq = 16
    inv_freq = ROPE_THETA ** (-jnp.arange(n_freq, dtype=F32) / n_freq)
    cos_parts, sin_parts = [], []
    if n_lat:
        rows = n_lat // GRID_W
        row = jnp.repeat(jnp.arange(rows, dtype=F32), GRID_W)
        col = jnp.tile(jnp.arange(GRID_W, dtype=F32), rows)
        ang = jnp.concatenate([row[:, None] * inv_freq, col[:, None] * inv_freq], axis=-1)
        cos_parts.append(jnp.cos(ang))
        sin_parts.append(jnp.sin(ang))
    if n_ctx:
        cos_parts.append(jnp.ones((n_ctx, 32), F32))
        sin_parts.append(jnp.zeros((n_ctx, 32), F32))
    cos = jnp.concatenate(cos_parts, axis=0)
    sin = jnp.concatenate(sin_parts, axis=0)
    cos_f = jnp.tile(jnp.concatenate([cos, cos], axis=-1), (1, 2))
    sin_s = jnp.tile(jnp.concatenate([-sin, sin], axis=-1), (1, 2))
    return cos_f, sin_s


def _rope(x, cos_f, sin_s, first_half):
    swapped = jnp.where(first_half, pltpu.roll(x, 96, 1), pltpu.roll(x, 32, 1))
    return x * cos_f + swapped * sin_s


KEY_SPLIT = 2


def _softmax_pv(q, k_scr, v_scr, n_k):
    h = n_k // KEY_SPLIT
    ss = [_dot_nt(q, k_scr[i * h:(i + 1) * h, :]).astype(BF16) for i in range(KEY_SPLIT)]
    m = jnp.max(ss[0], axis=-1, keepdims=True)
    for s in ss[1:]:
        m = jnp.maximum(m, jnp.max(s, axis=-1, keepdims=True))
    acc = _dot(jnp.exp2(ss[0] - m), v_scr[0:h, :])
    for i in range(1, KEY_SPLIT):
        acc = acc + _dot(jnp.exp2(ss[i] - m), v_scr[i * h:(i + 1) * h, :])
    return acc[:, :LANE], acc[:, LANE:]


def _diff_attn_kernel(n_seg, seg_rows, tq, *refs):
    lam_ref, q_ref = refs[0], refs[1]
    k_refs = refs[2:2 + n_seg]
    v_refs = refs[2 + n_seg:2 + 2 * n_seg]
    (gq_ref, gk_ref, go_ref, cq_ref, sq_ref, ck_ref, sk_ref, o_ref,
     k_scr, v_scr, q0_scr, q1_scr) = refs[2 + 2 * n_seg:]
    n_k = sum(seg_rows)

    lane = lax.broadcasted_iota(jnp.int32, (1, LANE), 1)
    first_half = (lane % 64) < 32
    comp0 = lane < 64

    def head_norm(x, g):
        sq = x * x
        s0 = jnp.sum(jnp.where(comp0, sq, 0.0), axis=-1, keepdims=True)
        s1 = jnp.sum(sq, axis=-1, keepdims=True) - s0
        ms = jnp.where(comp0, s0, s1) * (1.0 / 64.0)
        return x * lax.rsqrt(ms + EPS) * g

    @pl.when(pl.program_id(2) == 0)
    def _():
        r0 = 0
        for s in range(n_seg):
            n = seg_rows[s]
            k = head_norm(k_refs[s][...].astype(F32), gk_ref[...])
            k = _rope(k, ck_ref[r0:r0 + n, :], sk_ref[r0:r0 + n, :], first_half)
            k_scr[r0:r0 + n, :] = k.astype(BF16)
            v_scr[r0:r0 + n, :LANE] = v_refs[s][...]
            v_scr[r0:r0 + n, LANE:] = jnp.ones((n, LANE), BF16)
            r0 += n
        q = head_norm(q_ref[...].astype(F32), gq_ref[...])
        q = _rope(q, cq_ref[...], sq_ref[...], first_half) * (64.0 ** -0.5 * LOG2E)
        q0_scr[...] = jnp.where(comp0, q, 0.0).astype(BF16)
        q1_scr[...] = jnp.where(comp0, 0.0, q).astype(BF16)

    sub = min(tq, TQ_SUB)
    for j in range(tq // sub):
        row = pl.multiple_of(pl.program_id(2) * tq + j * sub, sub)
        o0, l0 = _softmax_pv(q0_scr[pl.ds(row, sub), :], k_scr, v_scr, n_k)
        o1, l1 = _softmax_pv(q1_scr[pl.ds(row, sub), :], k_scr, v_scr, n_k)
        o = o0 * (1.0 / l0) - o1 * (lam_ref[0] / l1)
        ms = jnp.mean(o * o, axis=-1, keepdims=True)
        o_ref[j * sub:(j + 1) * sub, :] = (o * lax.rsqrt(ms + EPS) * go_ref[...]).astype(BF16)


def diff_attention(z, lam, gq, gk, go, cos_f, sin_s, n_q, q_blk0, segs, tq):
    n_seg = len(segs)
    seg_rows = tuple(r for r, _ in segs)
    n_k = sum(seg_rows)
    nqb = n_q // tq
    hq, hk, hv = Z_DQ // LANE, Z_DK // LANE, Z_DV // LANE

    def kspec(rows, off, col0):
        return pl.BlockSpec((rows, LANE), lambda b, h, qb: (off + b, col0 + h))

    vec = pl.BlockSpec((1, LANE), lambda b, h, qb: (0, 0))
    qtab = pl.BlockSpec((n_q, LANE), lambda b, h, qb: (0, 0))
    ktab = pl.BlockSpec((n_k, LANE), lambda b, h, qb: (0, 0))
    in_specs = ([pl.BlockSpec(memory_space=pltpu.SMEM), kspec(n_q, q_blk0, hq)]
                + [kspec(r, off, hk) for r, off in segs]
                + [kspec(r, off, hv) for r, off in segs]
                + [vec, vec, vec, qtab, qtab, ktab, ktab])
    return pl.pallas_call(
        functools.partial(_diff_attn_kernel, n_seg, seg_rows, tq),
        grid=(BATCH, HEADS, nqb),
        in_specs=in_specs,
        out_specs=pl.BlockSpec((tq, LANE), lambda b, h, qb: (b * nqb + qb, h)),
        out_shape=jax.ShapeDtypeStruct((BATCH * n_q, HEADS * LANE), BF16),
        scratch_shapes=[pltpu.VMEM((n_k, LANE), BF16), pltpu.VMEM((n_k, 2 * LANE), BF16),
                        pltpu.VMEM((n_q, LANE), BF16), pltpu.VMEM((n_q, LANE), BF16)],
        compiler_params=_cp(3), name="diff_attention",
    )(lam, z, *([z] * (2 * n_seg)), gq, gk, go, cos_f, sin_s, cos_f, sin_s)


def _mla_up_kernel(cq_ref, ckv_ref, gq_ref, gkv_ref, wq_ref, wkv_ref, q_ref, kv_ref):
    def norm(x, g):
        ms = jnp.mean(x * x, axis=-1, keepdims=True)
        return (x * lax.rsqrt(ms + EPS) * g).astype(BF16)

    q_ref[...] = _dot(norm(cq_ref[...].astype(F32), gq_ref[...]), wq_ref[...]).astype(BF16)
    kv_ref[...] = _dot(norm(ckv_ref[...].astype(F32), gkv_ref[...]), wkv_ref[...]).astype(BF16)


def mla_up(z, g_cq, g_ckv, w_uq, w_ukv):
    tm = 1024
    n = HEADS * 256
    return pl.pallas_call(
        _mla_up_kernel,
        grid=(N_TOK // tm,),
        in_specs=[pl.BlockSpec((tm, MLA_Q_LORA), lambda i: (i, Z_CQ // MLA_Q_LORA)),
                  pl.BlockSpec((tm, MLA_KV_LORA), lambda i: (i, Z_CKV // MLA_KV_LORA)),
                  pl.BlockSpec((1, MLA_Q_LORA), lambda i: (0, 0)),
                  pl.BlockSpec((1, MLA_KV_LORA), lambda i: (0, 0)),
                  pl.BlockSpec((MLA_Q_LORA, n), lambda i: (0, 0)),
                  pl.BlockSpec((MLA_KV_LORA, n), lambda i: (0, 0))],
        out_specs=[pl.BlockSpec((tm, n), lambda i: (i, 0))] * 2,
        out_shape=[jax.ShapeDtypeStruct((N_TOK, n), BF16)] * 2,
        compiler_params=_cp(1), name="mla_up",
    )(z, z, g_cq, g_ckv, w_uq, w_ukv)


def _mla_attn_kernel(n_seg, seg_rows, tq, *refs):
    q_ref = refs[0]
    kv_refs = refs[1:1 + n_seg]
    kr_refs = refs[1 + n_seg:1 + 2 * n_seg]
    gq_ref, gk_ref, cq_ref, sq_ref, ck_ref, sk_ref, o_ref, k_scr, v_scr, q_scr = refs[1 + 2 * n_seg:]
    n_k = sum(seg_rows)

    lane = lax.broadcasted_iota(jnp.int32, (1, LANE), 1)
    first_half = (lane % 64) < 32

    def qk_norm(nope, rope, g_ref, scale):
        ss = jnp.sum(nope * nope, axis=-1, keepdims=True) + jnp.sum(rope * rope, axis=-1, keepdims=True)
        inv = lax.rsqrt(ss * (1.0 / MLA_QK) + EPS) * scale
        return ((nope * inv * g_ref[:, :LANE]).astype(BF16), (rope * inv * g_ref[:, LANE:]).astype(BF16))

    @pl.when(pl.program_id(2) == 0)
    def _():
        r0 = 0
        for s in range(n_seg):
            n = seg_rows[s]
            kv = kv_refs[s][...]
            rope = _rope(kr_refs[s][...].astype(F32), ck_ref[r0:r0 + n, :], sk_ref[r0:r0 + n, :], first_half)
            kn, kr = qk_norm(kv[:, :LANE].astype(F32), rope, gk_ref, 1.0)
            k_scr[r0:r0 + n, :LANE] = kn
            k_scr[r0:r0 + n, LANE:] = kr
            v_scr[r0:r0 + n, :LANE] = kv[:, LANE:]
            v_scr[r0:r0 + n, LANE:] = jnp.ones((n, LANE), BF16)
            r0 += n
        q = q_ref[...]
        rope = _rope(q[:, LANE:].astype(F32), cq_ref[...], sq_ref[...], first_half)
        qn, qr = qk_norm(q[:, :LANE].astype(F32), rope, gq_ref, MLA_QK ** -0.5 * LOG2E)
        q_scr[:, :LANE] = qn
        q_scr[:, LANE:] = qr

    sub = min(tq, TQ_SUB)
    for j in range(tq // sub):
        row = pl.multiple_of(pl.program_id(2) * tq + j * sub, sub)
        o, l = _softmax_pv(q_scr[pl.ds(row, sub), :], k_scr, v_scr, n_k)
        o_ref[j * sub:(j + 1) * sub, :] = (o * (1.0 / l)).astype(BF16)


def mla_attention(q_mla, kv, z, gq, gk, cos_f, sin_s, n_q, q_blk0, segs, tq):
    n_seg = len(segs)
    seg_rows = tuple(r for r, _ in segs)
    n_k = sum(seg_rows)
    nqb = n_q // tq
    ckr = Z_KR // LANE
    vec = pl.BlockSpec((1, 2 * LANE), lambda b, h, qb: (0, 0))
    qtab = pl.BlockSpec((n_q, LANE), lambda b, h, qb: (0, 0))
    ktab = pl.BlockSpec((n_k, LANE), lambda b, h, qb: (0, 0))
    in_specs = ([pl.BlockSpec((n_q, 2 * LANE), lambda b, h, qb: (q_blk0 + b, h))]
                + [pl.BlockSpec((r, 2 * LANE), (lambda off: lambda b, h, qb: (off + b, h))(off)) for r, off in segs]
                + [pl.BlockSpec((r, LANE), (lambda off: lambda b, h, qb: (off + b, ckr))(off)) for r, off in segs]
                + [vec, vec, qtab, qtab, ktab, ktab])
    return pl.pallas_call(
        functools.partial(_mla_attn_kernel, n_seg, seg_rows, tq),
        grid=(BATCH, HEADS, nqb),
        in_specs=in_specs,
        out_specs=pl.BlockSpec((tq, LANE), lambda b, h, qb: (b * nqb + qb, h)),
        out_shape=jax.ShapeDtypeStruct((BATCH * n_q, HEADS * LANE), BF16),
        scratch_shapes=[pltpu.VMEM((n_k, 2 * LANE), BF16), pltpu.VMEM((n_k, 2 * LANE), BF16),
                        pltpu.VMEM((n_q, 2 * LANE), BF16)],
        compiler_params=_cp(3), name="mla_attention",
    )(q_mla, *([kv] * n_seg), *([z] * n_seg), gq, gk, cos_f, sin_s, cos_f, sin_s)


def _merge_kernel(yf_ref, od_ref, om_ref, g0_ref, g1_ref, g2_ref, wf_ref, wd_ref, wm_ref, o_ref):
    f = _dot(yf_ref[...], wf_ref[...])
    d = _dot(od_ref[...], wd_ref[...])
    m = _dot(om_ref[...], wm_ref[...])
    o = (_sigmoid(g0_ref[...].astype(F32)) * f + _sigmoid(g1_ref[...].astype(F32)) * d
         + _sigmoid(g2_ref[...].astype(F32)) * m)
    o_ref[...] = o.astype(BF16)


def merge_branches(yf, od, om, z, w_f, w_d, w_m, n_rows):
    tm, tn = TM, 512
    k = FNET_W
    a_spec = pl.BlockSpec((tm, k), lambda i, j: (i, 0))
    w_spec = pl.BlockSpec((k, tn), lambda i, j: (0, j))
    g0 = Z_GATE // tn

    def gspec(r):
        return pl.BlockSpec((tm, tn), lambda i, j: (i, g0 + r * (D // tn) + j))

    return pl.pallas_call(
        _merge_kernel,
        grid=(n_rows // tm, D // tn),
        in_specs=[a_spec, a_spec, a_spec, gspec(0), gspec(1), gspec(2), w_spec, w_spec, w_spec],
        out_specs=pl.BlockSpec((tm, tn), lambda i, j: (i, j)),
        out_shape=jax.ShapeDtypeStruct((n_rows, D), BF16),
        compiler_params=_cp(2), name="merge_branches",
    )(yf, od, om, z, z, z, w_f, w_d, w_m)


def _split3(x):
    hi = x.astype(BF16)
    r = x - hi.astype(F32)
    mid = r.astype(BF16)
    lo = (r - mid.astype(F32)).astype(BF16)
    return hi, mid, lo


def _router_kernel(x_ref, g_ref, sh_ref, sc_ref, wh_ref, wm_ref, wl_ref, u_ref, rt_ref):
    u = _modulate(x_ref[...], g_ref[...], sh_ref[0], sc_ref[0])
    u_ref[...] = u
    uh, um, ul = _split3(u)
    wh, wm, wl = wh_ref[...], wm_ref[...], wl_ref[...]
    logits = (_dot(ul, wh) + _dot(uh, wl) + _dot(um, wm)) + (_dot(um, wh) + _dot(uh, wm)) + _dot(uh, wh)
    lane = lax.broadcasted_iota(jnp.int32, logits.shape, 1).astype(F32)
    neg = -jnp.inf
    lg = jnp.where(lane < N_EXPERTS, logits, neg)
    m1 = jnp.max(lg, axis=-1, keepdims=True)
    i1 = jnp.min(jnp.where(lg == m1, lane, float(LANE)), axis=-1, keepdims=True)
    lg2 = jnp.where(lane == i1, neg, lg)
    m2 = jnp.max(lg2, axis=-1, keepdims=True)
    i2 = jnp.min(jnp.where(lg2 == m2, lane, float(LANE)), axis=-1, keepdims=True)
    p1 = 1.0 / (1.0 + jnp.exp(m2 - m1))
    p2 = 1.0 - p1
    rt = jnp.where(lane == 0.0, i1, jnp.where(lane == 1.0, i2, jnp.where(lane == 2.0, p1,
                                                                      jnp.where(lane == 3.0, p2, 0.0))))
    rt_ref[...] = rt


def moe_router(x, g, mod, chunk, w_router, n_rows):
    tm = 512
    row = _mod_row(tm)
    wpad = jnp.pad(w_router, ((0, 0), (0, LANE - N_EXPERTS)))
    wh, wm, wl = _split3(wpad)
    wspec = pl.BlockSpec((D, LANE), lambda i: (0, 0))
    return pl.pallas_call(
        _router_kernel,
        grid=(n_rows // tm,),
        in_specs=[pl.BlockSpec((tm, D), lambda i: (i, 0)),
                  pl.BlockSpec((1, D), lambda i: (0, 0)),
                  pl.BlockSpec((1, 1, D), lambda i: (row(i), 0, chunk)),
                  pl.BlockSpec((1, 1, D), lambda i: (row(i), 0, chunk + 1)),
                  wspec, wspec, wspec],
        out_specs=[pl.BlockSpec((tm, D), lambda i: (i, 0)), pl.BlockSpec((tm, LANE), lambda i: (i, 0))],
        out_shape=[jax.ShapeDtypeStruct((n_rows, D), F32), jax.ShapeDtypeStruct((n_rows, LANE), F32)],
        compiler_params=_cp(1), name="moe_router",
    )(x, g, mod, mod, wh, wm, wl)


def _row_copy(src_hbm, dst_ref, sem, src_row, dst_row):
    return pltpu.make_async_copy(src_hbm.at[pl.ds(src_row, 1), :], dst_ref.at[pl.ds(dst_row, 1), :], sem)


def _gather_kernel(idx_ref, src_hbm, o_ref, sem):
    base = pl.program_id(0) * TG

    def issue(r, c):
        _row_copy(src_hbm, o_ref, sem, idx_ref[base + r], r).start()
        return c

    lax.fori_loop(0, TG, issue, 0, unroll=8)

    def wait(r, c):
        _row_copy(src_hbm, o_ref, sem, 0, r).wait()
        return c

    lax.fori_loop(0, TG, wait, 0, unroll=8)


def gather_rows(src, idx):
    n_out = idx.shape[0]
    return pl.pallas_call(
        _gather_kernel,
        grid_spec=pltpu.PrefetchScalarGridSpec(
            num_scalar_prefetch=1, grid=(n_out // TG,),
            in_specs=[pl.BlockSpec(memory_space=pl.ANY)],
            out_specs=pl.BlockSpec((TG, D), lambda i, idx: (i, 0)),
            scratch_shapes=[pltpu.SemaphoreType.DMA]),
        out_shape=jax.ShapeDtypeStruct((n_out, D), src.dtype),
        compiler_params=_cp(1), name="gather_rows",
    )(idx, src)


def _moe_kernel(te_ref, nu_ref, xs_ref, w1_ref, w3_ref, w2_ref, o_ref, xb_scr, acc_scr):
    i, j = pl.program_id(0), pl.program_id(1)
    last = pl.num_programs(1) - 1
    used = i < nu_ref[0]

    @pl.when(used)
    def _():
        @pl.when(j == 0)
        def _():
            xb_scr[...] = xs_ref[...].astype(BF16)
            acc_scr[...] = jnp.zeros_like(acc_scr)

        xb = xb_scr[...]
        a = _dot(xb, w1_ref[0])
        b = _dot(xb, w3_ref[0])
        h = (a * _sigmoid(a) * b).astype(BF16)
        acc_scr[...] += _dot(h, w2_ref[0])

        @pl.when(j == last)
        def _():
            o_ref[...] = acc_scr[...]

    @pl.when(jnp.logical_and(jnp.logical_not(used), j == last))
    def _():
        o_ref[...] = jnp.zeros_like(o_ref)


def moe_experts(xs, tile_expert, n_used, w1, w3, w2):
    p = xs.shape[0]
    tm, tf = TM_MOE, TF_MOE
    return pl.pallas_call(
        _moe_kernel,
        grid_spec=pltpu.PrefetchScalarGridSpec(
            num_scalar_prefetch=2, grid=(p // tm, D_FF_EXPERT // tf),
            in_specs=[pl.BlockSpec((tm, D), lambda i, j, te, nu: (i, 0)),
                      pl.BlockSpec((1, D, tf), lambda i, j, te, nu: (te[i], 0, j)),
                      pl.BlockSpec((1, D, tf), lambda i, j, te, nu: (te[i], 0, j)),
                      pl.BlockSpec((1, tf, D), lambda i, j, te, nu: (te[i], j, 0))],
            out_specs=pl.BlockSpec((tm, D), lambda i, j, te, nu: (i, 0)),
            scratch_shapes=[pltpu.VMEM((tm, D), BF16), pltpu.VMEM((tm, D), F32)]),
        out_shape=jax.ShapeDtypeStruct((p, D), F32),
        compiler_params=_cp(2), name="moe_experts",
    )(tile_expert, n_used, xs, w1, w3, w2)


def _combine_kernel(n_tok, pos_ref, ys_hbm, x_ref, rt_ref, gate_ref, o_ref, buf, sem):
    base = pl.program_id(0) * TG

    def issue(r, c):
        _row_copy(ys_hbm, buf.at[0], sem, pos_ref[base + r], r).start()
        _row_copy(ys_hbm, buf.at[1], sem, pos_ref[n_tok + base + r], r).start()
        return c

    lax.fori_loop(0, TG, issue, 0, unroll=8)

    def wait(r, c):
        _row_copy(ys_hbm, buf.at[0], sem, 0, r).wait()
        _row_copy(ys_hbm, buf.at[1], sem, 0, r).wait()
        return c

    lax.fori_loop(0, TG, wait, 0, unroll=8)
    rt = rt_ref[...]
    y = rt[:, 2:3] * buf[0] + rt[:, 3:4] * buf[1]
    o_ref[...] = x_ref[...] + gate_ref[0] * y


def moe_combine(ys, pos, x, rt, mod, chunk, n_rows):
    row = _mod_row(TG)
    return pl.pallas_call(
        functools.partial(_combine_kernel, n_rows),
        grid_spec=pltpu.PrefetchScalarGridSpec(
            num_scalar_prefetch=1, grid=(n_rows // TG,),
            in_specs=[pl.BlockSpec(memory_space=pl.ANY),
                      pl.BlockSpec((TG, D), lambda i, pos: (i, 0)),
                      pl.BlockSpec((TG, LANE), lambda i, pos: (i, 0)),
                      pl.BlockSpec((1, 1, D), lambda i, pos: (row(i), 0, chunk))],
            out_specs=pl.BlockSpec((TG, D), lambda i, pos: (i, 0)),
            scratch_shapes=[pltpu.VMEM((2, TG, D), F32), pltpu.SemaphoreType.DMA]),
        out_shape=jax.ShapeDtypeStruct((n_rows, D), F32),
        compiler_params=_cp(1), name="moe_combine",
    )(pos, ys, x, rt, mod)


def moe_ffn(x, g, mod, w_router, w1, w3, w2, n_rows):
    u, rt = moe_router(x, g, mod, 3, w_router, n_rows)
    tm = TM_MOE
    n_pairs = 2 * n_rows
    p_rows = n_pairs + N_EXPERTS * tm
    n_tiles = p_rows // tm
    e_flat = jnp.concatenate([rt[:, 0], rt[:, 1]]).astype(jnp.int32)
    onehot = (e_flat[:, None] == jnp.arange(N_EXPERTS, dtype=jnp.int32)[None, :]).astype(jnp.int32)
    csum = jnp.cumsum(onehot, axis=0)
    counts = csum[-1]
    rank = jnp.sum((csum - onehot) * onehot, axis=1)
    padded = ((counts + tm - 1) // tm) * tm
    g_end = jnp.cumsum(padded)
    g_start = g_end - padded
    dest = g_start[e_flat] + rank
    tok = jnp.tile(jnp.arange(n_rows, dtype=jnp.int32), 2)
    src = jnp.zeros((p_rows,), jnp.int32).at[dest].set(tok)
    tile_start = jnp.arange(n_tiles, dtype=jnp.int32) * tm
    tile_expert = jnp.minimum(jnp.sum((tile_start[:, None] >= g_end[None, :]).astype(jnp.int32), axis=1),
                              N_EXPERTS - 1).astype(jnp.int32)
    n_used = (g_end[-1] // tm).astype(jnp.int32).reshape(1)
    xs = gather_rows(u, src)
    ys = moe_experts(xs, tile_expert, n_used, w1, w3, w2)
    return moe_combine(ys, dest, x, rt, mod, 5, n_rows)


def kernel(x, c, ctx, c_ctx, w_mod, b_mod, g_norm1, g_norm2, w_in, g_diff_q, g_diff_k, diff_lambda,
           g_diff_out, w_diff_o, g_mla_cq, g_mla_ckv, w_mla_uq, w_mla_ukv, g_mla_q, g_mla_k, w_mla_o,
           w_fnet, w_out, ffn_w1, ffn_w3, ffn_w2, moe_router, moe_w1, moe_w3, moe_w2):
    xs = jnp.concatenate([x.reshape(N_LAT, D), ctx.reshape(N_CTX, D)], axis=0)
    c_all = jnp.zeros((32, D), F32).at[:BATCH].set(c).at[BATCH].set(c_ctx)
    mods = mod_tables(c_all, w_mod, b_mod)

    cos_lat, sin_lat = rope_tables(SEQ, CTX)
    cos_ctx, sin_ctx = rope_tables(0, CTX)
    cw, sw = dft_matrices(FNET_GROUP_W, FNET_GROUP_W ** -0.5)
    cn_lat, sn_lat = dft_matrices(SEQ, SEQ ** -0.5)
    cn_ctx, sn_ctx = dft_matrices(CTX, CTX ** -0.5)
    sn_lat, sn_ctx = -sn_lat, -sn_ctx

    lat_segs = [(SEQ, 0), (CTX, N_LAT // CTX)]
    ctx_segs = [(CTX, N_LAT // CTX)]

    for i in range(DEPTH):
        need_ctx = i < DEPTH - 1
        n_rows = N_TOK if need_ctx else N_LAT
        mod = mods[i].reshape(32, 1, N_MOD * D)
        wi = w_in[i]
        w_in_p = jnp.concatenate(
            [wi[:, :Z_KR], wi[:, Z_KR:Z_KR + MLA_ROPE], jnp.zeros((D, Z_GATE - Z_KR - MLA_ROPE), F32),
             wi[:, Z_KR + MLA_ROPE:]], axis=1).astype(BF16)
        w_uq = jnp.pad(w_mla_uq[i].reshape(MLA_Q_LORA, HEADS, MLA_QK),
                       ((0, 0), (0, 0), (0, 256 - MLA_QK))).reshape(MLA_Q_LORA, HEADS * 256).astype(BF16)
        w_ukv = w_mla_ukv[i].astype(BF16)
        g_q = jnp.pad(g_mla_q[i], (0, 256 - MLA_QK)).reshape(1, 256)
        g_k = jnp.pad(g_mla_k[i], (0, 256 - MLA_QK)).reshape(1, 256)
        lam_init = 0.8 - 0.6 * math.exp(-0.3 * i)
        lp = diff_lambda[i]
        lam = (jnp.exp(jnp.sum(lp[0] * lp[1])) - jnp.exp(jnp.sum(lp[2] * lp[3])) + lam_init).reshape(1)
        gdq = jnp.tile(g_diff_q[i], 2).reshape(1, LANE)
        gdk = jnp.tile(g_diff_k[i], 2).reshape(1, LANE)
        gdo = (g_diff_out[i] * (1.0 - lam_init)).reshape(1, LANE)

        z = norm_matmul(xs, g_norm1[i].reshape(1, D), mod, 0, [w_in_p], 1024, N_TOK)
        zc, zs = dft_channels(z, cw, sw)
        yf = dft_positions(cn_lat, sn_lat, zc, zs, SEQ, 0, 512)
        q_mla, kv = mla_up(z, g_mla_cq[i].reshape(1, -1), g_mla_ckv[i].reshape(1, -1), w_uq, w_ukv)
        od = diff_attention(z, lam, gdq, gdk, gdo, cos_lat, sin_lat, SEQ, 0, lat_segs, TQ)
        om = mla_attention(q_mla, kv, z, g_q, g_k, cos_lat, sin_lat, SEQ, 0, lat_segs, TQ)
        if need_ctx:
            yf = jnp.concatenate([yf, dft_positions(cn_ctx, sn_ctx, zc, zs, CTX, N_LAT // CTX, CTX)], axis=0)
            od = jnp.concatenate(
                [od, diff_attention(z, lam, gdq, gdk, gdo, cos_ctx, sin_ctx, CTX, N_LAT // CTX, ctx_segs, CTX)],
                axis=0)
            om = jnp.concatenate(
                [om, mla_attention(q_mla, kv, z, g_q, g_k, cos_ctx, sin_ctx, CTX, N_LAT // CTX, ctx_segs, CTX)],
                axis=0)
        merged = merge_branches(yf, od, om, z, w_fnet[i].astype(BF16), w_diff_o[i].astype(BF16),
                                w_mla_o[i].astype(BF16), n_rows)
        xs = matmul_residual(merged, w_out[i].astype(BF16), xs, mod, 2, TM, 1024, n_rows)

        g2 = g_norm2[i].reshape(1, D)
        if i % 2 == 0:
            j = i // 2
            h = norm_matmul(xs, g2, mod, 3, [ffn_w1[j].astype(BF16), ffn_w3[j].astype(BF16)], 512, n_rows)
            xs = matmul_residual(h, ffn_w2[j].astype(BF16), xs, mod, 5, TM_FFN2, 512, n_rows)
        else:
            j = i // 2
            xs = moe_ffn(xs, g2, mod, moe_router[j], moe_w1[j].astype(BF16), moe_w3[j].astype(BF16),
                         moe_w2[j].astype(BF16), n_rows)
    return xs[:N_LAT].reshape(BATCH, SEQ, D)
```

```python
import functools
import math

import jax
import jax.numpy as jnp
from jax import lax
from jax.experimental import pallas as pl
from jax.experimental.pallas import tpu as pltpu

F32 = jnp.float32
BF16 = jnp.bfloat16

D = 2048
BATCH = 16
SEQ = 2048
DEPTH = 4
GRID_W = 64
CTX = 256
N_LAT = BATCH * SEQ
N_CTX = BATCH * CTX
N_TOK = N_LAT + N_CTX
HEADS = 8
FNET_W = 1024
FNET_GROUP_W = 256
MLA_Q_LORA = 512
MLA_KV_LORA = 256
MLA_NOPE = 128
MLA_ROPE = 64
MLA_QK = MLA_NOPE + MLA_ROPE
D_FF_DENSE = 5632
N_EXPERTS = 8
D_FF_EXPERT = 4096
ROPE_THETA = 10000.0
EPS = 1e-6
N_MOD = 6
LOG2E = math.log2(math.e)

Z_F, Z_DQ, Z_DK, Z_DV, Z_CQ, Z_CKV, Z_KR, Z_GATE = 0, 1024, 2048, 3072, 4096, 4608, 4864, 5120
Z_COLS = Z_GATE + 3 * D

LANE = 128
VMEM_LIMIT = 56 * 1024 * 1024

TM = 1024
TM_FFN2 = 512
TQ_DIFF, SUB_DIFF = 2048, 256
TQ_MLA, SUB_MLA = 2048, 512
TM_MOE = 512
TF_MOE = 512
TG = 256


def _cp(n_axes):
    return pltpu.CompilerParams(dimension_semantics=("arbitrary",) * n_axes,
                                vmem_limit_bytes=VMEM_LIMIT)


def _dot(a, b):
    return jnp.dot(a, b, preferred_element_type=F32)


def _dot_nt(a, b):
    return lax.dot_general(a, b, (((1,), (1,)), ((), ())), preferred_element_type=F32)


def _sigmoid(x):
    return 1.0 / (1.0 + jnp.exp(-x))


def _mod_row(tm):
    return lambda i: jnp.minimum((i * tm) // SEQ, BATCH)


def _mod_kernel(c_ref, w_ref, b_ref, o_ref):
    c = c_ref[...]
    a = (c * _sigmoid(c)).astype(BF16)
    o_ref[0] = _dot(a, w_ref[0].astype(BF16)) + b_ref[0]


def mod_tables(c_all, w_mod, b_mod):
    tn = 1024
    rows = c_all.shape[0]
    return pl.pallas_call(
        _mod_kernel,
        grid=(DEPTH, N_MOD * D // tn),
        in_specs=[pl.BlockSpec((rows, D), lambda l, j: (0, 0)),
                  pl.BlockSpec((1, D, tn), lambda l, j: (l, 0, j)),
                  pl.BlockSpec((1, 1, tn), lambda l, j: (l, 0, j))],
        out_specs=pl.BlockSpec((1, rows, tn), lambda l, j: (l, 0, j)),
        out_shape=jax.ShapeDtypeStruct((DEPTH, rows, N_MOD * D), F32),
        compiler_params=_cp(2), name="mod_tables",
    )(c_all, w_mod, b_mod.reshape(DEPTH, 1, N_MOD * D))


def _modulate(x, g, shift, scale):
    ms = jnp.mean(x * x, axis=-1, keepdims=True)
    y = x * lax.rsqrt(ms + EPS) * g
    return y * (1.0 + scale) + shift


def _norm_mm_kernel(x_ref, g_ref, sh_ref, sc_ref, w_ref, o_ref, u_scr):
    @pl.when(pl.program_id(1) == 0)
    def _():
        u_scr[...] = _modulate(x_ref[...], g_ref[...], sh_ref[0], sc_ref[0]).astype(BF16)

    o_ref[...] = _dot(u_scr[...], w_ref[...]).astype(o_ref.dtype)


def _norm_swiglu_kernel(x_ref, g_ref, sh_ref, sc_ref, w1_ref, w3_ref, o_ref, u_scr):
    @pl.when(pl.program_id(1) == 0)
    def _():
        u_scr[...] = _modulate(x_ref[...], g_ref[...], sh_ref[0], sc_ref[0]).astype(BF16)

    u = u_scr[...]
    a = _dot(u, w1_ref[...])
    b = _dot(u, w3_ref[...])
    o_ref[...] = (a * _sigmoid(a) * b).astype(o_ref.dtype)


def norm_matmul(x, g, mod, chunk, ws, tn, n_rows):
    n = ws[0].shape[1]
    tm = TM
    kern = _norm_mm_kernel if len(ws) == 1 else _norm_swiglu_kernel
    row = _mod_row(tm)
    return pl.pallas_call(
        kern,
        grid=(n_rows // tm, n // tn),
        in_specs=[pl.BlockSpec((tm, D), lambda i, j: (i, 0)),
                  pl.BlockSpec((1, D), lambda i, j: (0, 0)),
                  pl.BlockSpec((1, 1, D), lambda i, j: (row(i), 0, chunk)),
                  pl.BlockSpec((1, 1, D), lambda i, j: (row(i), 0, chunk + 1))]
                 + [pl.BlockSpec((D, tn), lambda i, j: (0, j)) for _ in ws],
        out_specs=pl.BlockSpec((tm, tn), lambda i, j: (i, j)),
        out_shape=jax.ShapeDtypeStruct((n_rows, n), BF16),
        scratch_shapes=[pltpu.VMEM((tm, D), BF16)],
        compiler_params=_cp(2), name="norm_matmul%d" % len(ws),
    )(x, g, mod, mod, *ws)


def _mm_res_kernel(a_ref, w_ref, x_ref, gate_ref, o_ref):
    o_ref[...] = x_ref[...] + gate_ref[0] * _dot(a_ref[...], w_ref[...])


def matmul_residual(a, w, x, mod, chunk, tm, tn, n_rows):
    k = a.shape[1]
    row = _mod_row(tm)
    nj = D // tn
    return pl.pallas_call(
        _mm_res_kernel,
        grid=(n_rows // tm, nj),
        in_specs=[pl.BlockSpec((tm, k), lambda i, j: (i, 0)),
                  pl.BlockSpec((k, tn), lambda i, j: (0, j)),
                  pl.BlockSpec((tm, tn), lambda i, j: (i, j)),
                  pl.BlockSpec((1, 1, tn), lambda i, j: (row(i), 0, chunk * nj + j))],
        out_specs=pl.BlockSpec((tm, tn), lambda i, j: (i, j)),
        out_shape=jax.ShapeDtypeStruct((n_rows, D), F32),
        compiler_params=_cp(2), name="matmul_residual",
    )(a, w, x, mod)


def _dft_ch_kernel(z_ref, cw_ref, sw_ref, zc_ref, zs_ref):
    z = z_ref[...]
    zc_ref[...] = _dot(z, cw_ref[...]).astype(BF16)
    zs_ref[...] = _dot(z, sw_ref[...]).astype(BF16)


def dft_channels(z, cw, sw):
    tm = 1024
    gw = FNET_GROUP_W
    spec = pl.BlockSpec((tm, gw), lambda i, g: (i, g))
    wspec = pl.BlockSpec((gw, gw), lambda i, g: (0, 0))
    return pl.pallas_call(
        _dft_ch_kernel,
        grid=(N_TOK // tm, FNET_W // gw),
        in_specs=[spec, wspec, wspec],
        out_specs=[spec, spec],
        out_shape=[jax.ShapeDtypeStruct((N_TOK, FNET_W), BF16)] * 2,
        compiler_params=_cp(2), name="dft_channels",
    )(z, cw, sw)


def _dft_pos_kernel(cn_ref, sn_ref, zc_ref, zs_ref, y_ref):
    y_ref[...] = (_dot(cn_ref[...], zc_ref[...]) + _dot(sn_ref[...], zs_ref[...])).astype(BF16)


def dft_positions(cn, sn, zc, zs, n, row_blk0, tm):
    mt = n // tm
    zspec = pl.BlockSpec((n, FNET_W), lambda b, m: (row_blk0 + b, 0))
    mspec = pl.BlockSpec((tm, n), lambda b, m: (m, 0))
    return pl.pallas_call(
        _dft_pos_kernel,
        grid=(BATCH, mt),
        in_specs=[mspec, mspec, zspec, zspec],
        out_specs=pl.BlockSpec((tm, FNET_W), lambda b, m: (b * mt + m, 0)),
        out_shape=jax.ShapeDtypeStruct((BATCH * n, FNET_W), BF16),
        compiler_params=_cp(2), name="dft_positions",
    )(cn, sn, zc, zs)


def dft_matrices(n, scale):
    idx = jnp.arange(n, dtype=jnp.int32)
    prod = (idx[:, None] * idx[None, :]) % n
    ang = prod.astype(F32) * (2.0 * math.pi / n)
    return (jnp.cos(ang) * scale).astype(BF16), (jnp.sin(ang) * scale).astype(BF16)


def rope_tables(n_lat, n_ctx):
    n_freq = 16
    inv_freq = ROPE_THETA ** (-jnp.arange(n_freq, dtype=F32) / n_freq)
    cos_parts, sin_parts = [], []
    if n_lat:
        rows = n_lat // GRID_W
        row = jnp.repeat(jnp.arange(rows, dtype=F32), GRID_W)
        col = jnp.tile(jnp.arange(GRID_W, dtype=F32), rows)
        ang = jnp.concatenate([row[:, None] * inv_freq, col[:, None] * inv_freq], axis=-1)
        cos_parts.append(jnp.cos(ang))
        sin_parts.append(jnp.sin(ang))
    if n_ctx:
        cos_parts.append(jnp.ones((n_ctx, 32), F32))
        sin_parts.append(jnp.zeros((n_ctx, 32), F32))
    cos = jnp.concatenate(cos_parts, axis=0)
    sin = jnp.concatenate(sin_parts, axis=0)
    cos_f = jnp.tile(jnp.concatenate([cos, cos], axis=-1), (1, 2))
    sin_s = jnp.tile(jnp.concatenate([-sin, sin], axis=-1), (1, 2))
    return cos_f, sin_s


def _rope(x, cos_f, sin_s, first_half):
    swapped = jnp.where(first_half, pltpu.roll(x, 96, 1), pltpu.roll(x, 32, 1))
    return x * cos_f + swapped * sin_s


def _softmax_pv(q, k_scr, v_scr):
    s = _dot_nt(q, k_scr[...]).astype(BF16)
    m = jnp.max(s, axis=-1, keepdims=True)
    acc = _dot(jnp.exp2(s - m), v_scr[...])
    return acc[:, :LANE], acc[:, LANE:]


def _softmax_pv_t(q, k_scr, vt_scr):
    s = _dot_nt(k_scr[...], q).astype(BF16)
    m = jnp.max(s, axis=0, keepdims=True)
    acc = _dot(vt_scr[...], jnp.exp2(s - m))
    return acc[:LANE, :], acc[LANE:, :]


def _diff_attn_kernel(n_seg, seg_rows, tq, *refs):
    lam_ref, q_ref = refs[0], refs[1]
    k_refs = refs[2:2 + n_seg]
    v_refs = refs[2 + n_seg:2 + 2 * n_seg]
    (gq_ref, gk_ref, go_ref, cq_ref, sq_ref, ck_ref, sk_ref, o_ref,
     k_scr, v_scr, q0_scr, q1_scr) = refs[2 + 2 * n_seg:]

    lane = lax.broadcasted_iota(jnp.int32, (1, LANE), 1)
    first_half = (lane % 64) < 32
    comp0 = lane < 64

    def head_norm(x, g):
        gi = lax.broadcasted_iota(jnp.int32, (LANE, LANE), 0) // 64
        gj = lax.broadcasted_iota(jnp.int32, (LANE, LANE), 1) // 64
        group_mean = jnp.where(gi == gj, 1.0 / 64.0, 0.0).astype(BF16)
        ms = _dot((x * x).astype(BF16), group_mean)
        return x * lax.rsqrt(ms + EPS) * g

    @pl.when(pl.program_id(2) == 0)
    def _():
        r0 = 0
        for s in range(n_seg):
            n = seg_rows[s]
            k = head_norm(k_refs[s][...].astype(F32), gk_ref[...])
            k = _rope(k, ck_ref[r0:r0 + n, :], sk_ref[r0:r0 + n, :], first_half)
            k_scr[r0:r0 + n, :] = k.astype(BF16)
            v_scr[r0:r0 + n, :LANE] = v_refs[s][...]
            v_scr[r0:r0 + n, LANE:] = jnp.ones((n, LANE), BF16)
            r0 += n
        q = head_norm(q_ref[...].astype(F32), gq_ref[...])
        q = _rope(q, cq_ref[...], sq_ref[...], first_half) * (64.0 ** -0.5 * LOG2E)
        q0_scr[...] = jnp.where(comp0, q, 0.0).astype(BF16)
        q1_scr[...] = jnp.where(comp0, 0.0, q).astype(BF16)

    sub = min(tq, SUB_DIFF)
    for j in range(tq // sub):
        row = pl.multiple_of(pl.program_id(2) * tq + j * sub, sub)
        o0, l0 = _softmax_pv(q0_scr[pl.ds(row, sub), :], k_scr, v_scr)
        o1, l1 = _softmax_pv(q1_scr[pl.ds(row, sub), :], k_scr, v_scr)
        o = o0 * (1.0 / l0) - o1 * (lam_ref[0] / l1)
        ms = jnp.mean(o * o, axis=-1, keepdims=True)
        o_ref[j * sub:(j + 1) * sub, :] = (o * lax.rsqrt(ms + EPS) * go_ref[...]).astype(BF16)


def diff_attention(z, lam, gq, gk, go, cos_f, sin_s, n_q, q_blk0, segs, tq):
    n_seg = len(segs)
    seg_rows = tuple(r for r, _ in segs)
    n_k = sum(seg_rows)
    tq = min(tq, n_q)
    nqb = n_q // tq
    hq, hk, hv = Z_DQ // LANE, Z_DK // LANE, Z_DV // LANE

    def kspec(rows, off, col0):
        return pl.BlockSpec((rows, LANE), lambda b, h, qb: (off + b, col0 + h))

    vec = pl.BlockSpec((1, LANE), lambda b, h, qb: (0, 0))
    qtab = pl.BlockSpec((n_q, LANE), lambda b, h, qb: (0, 0))
    ktab = pl.BlockSpec((n_k, LANE), lambda b, h, qb: (0, 0))
    in_specs = ([pl.BlockSpec(memory_space=pltpu.SMEM), kspec(n_q, q_blk0, hq)]
                + [kspec(r, off, hk) for r, off in segs]
                + [kspec(r, off, hv) for r, off in segs]
                + [vec, vec, vec, qtab, qtab, ktab, ktab])
    return pl.pallas_call(
        functools.partial(_diff_attn_kernel, n_seg, seg_rows, tq),
        grid=(BATCH, HEADS, nqb),
        in_specs=in_specs,
        out_specs=pl.BlockSpec((tq, LANE), lambda b, h, qb: (b * nqb + qb, h)),
        out_shape=jax.ShapeDtypeStruct((BATCH * n_q, HEADS * LANE), BF16),
        scratch_shapes=[pltpu.VMEM((n_k, LANE), BF16), pltpu.VMEM((n_k, 2 * LANE), BF16),
                        pltpu.VMEM((n_q, LANE), BF16), pltpu.VMEM((n_q, LANE), BF16)],
        compiler_params=_cp(3), name="diff_attention",
    )(lam, z, *([z] * (2 * n_seg)), gq, gk, go, cos_f, sin_s, cos_f, sin_s)


def _mla_up_kernel(cq_ref, ckv_ref, gq_ref, gkv_ref, wq_ref, wkv_ref, q_ref, kv_ref):
    def norm(x, g):
        ms = jnp.mean(x * x, axis=-1, keepdims=True)
        return (x * lax.rsqrt(ms + EPS) * g).astype(BF16)

    q_ref[...] = _dot(norm(cq_ref[...].astype(F32), gq_ref[...]), wq_ref[...]).astype(BF16)
    kv_ref[...] = _dot(norm(ckv_ref[...].astype(F32), gkv_ref[...]), wkv_ref[...]).astype(BF16)


def mla_up(z, g_cq, g_ckv, w_uq, w_ukv):
    tm = 1024
    n = HEADS * 256
    return pl.pallas_call(
        _mla_up_kernel,
        grid=(N_TOK // tm,),
        in_specs=[pl.BlockSpec((tm, MLA_Q_LORA), lambda i: (i, Z_CQ // MLA_Q_LORA)),
                  pl.BlockSpec((tm, MLA_KV_LORA), lambda i: (i, Z_CKV // MLA_KV_LORA)),
                  pl.BlockSpec((1, MLA_Q_LORA), lambda i: (0, 0)),
                  pl.BlockSpec((1, MLA_KV_LORA), lambda i: (0, 0)),
                  pl.BlockSpec((MLA_Q_LORA, n), lambda i: (0, 0)),
                  pl.BlockSpec((MLA_KV_LORA, n), lambda i: (0, 0))],
        out_specs=[pl.BlockSpec((tm, n), lambda i: (i, 0))] * 2,
        out_shape=[jax.ShapeDtypeStruct((N_TOK, n), BF16)] * 2,
        compiler_params=_cp(1), name="mla_up",
    )(z, z, g_cq, g_ckv, w_uq, w_ukv)


def _mla_attn_kernel(n_seg, seg_rows, tq, *refs):
    q_ref = refs[0]
    kv_refs = refs[1:1 + n_seg]
    kr_refs = refs[1 + n_seg:1 + 2 * n_seg]
    gq_ref, gk_ref, cq_ref, sq_ref, ck_ref, sk_ref, o_ref, k_scr, vt_scr, q_scr = refs[1 + 2 * n_seg:]

    lane = lax.broadcasted_iota(jnp.int32, (1, LANE), 1)
    first_half = (lane % 64) < 32

    def qk_norm(nope, rope, g_ref, scale):
        ones = jnp.ones((LANE, LANE), BF16)
        ss = _dot((nope * nope).astype(BF16), ones) + _dot((rope * rope).astype(BF16), ones)
        inv = lax.rsqrt(ss * (1.0 / MLA_QK) + EPS) * scale
        return ((nope * inv * g_ref[:, :LANE]).astype(BF16), (rope * inv * g_ref[:, LANE:]).astype(BF16))

    @pl.when(pl.program_id(2) == 0)
    def _():
        r0 = 0
        for s in range(n_seg):
            n = seg_rows[s]
            kv = kv_refs[s][...]
            rope = _rope(kr_refs[s][...].astype(F32), ck_ref[r0:r0 + n, :], sk_ref[r0:r0 + n, :], first_half)
            kn, kr = qk_norm(kv[:, :LANE].astype(F32), rope, gk_ref, 1.0)
            k_scr[r0:r0 + n, :LANE] = kn
            k_scr[r0:r0 + n, LANE:] = kr
            vt_scr[:LANE, r0:r0 + n] = kv[:, LANE:].astype(F32).T.astype(BF16)
            vt_scr[LANE:, r0:r0 + n] = jnp.ones((LANE, n), BF16)
            r0 += n
        q = q_ref[...]
        rope = _rope(q[:, LANE:].astype(F32), cq_ref[...], sq_ref[...], first_half)
        qn, qr = qk_norm(q[:, :LANE].astype(F32), rope, gq_ref, MLA_QK ** -0.5 * LOG2E)
        q_scr[:, :LANE] = qn
        q_scr[:, LANE:] = qr

    sub = min(tq, SUB_MLA)
    for j in range(tq // sub):
        row = pl.multiple_of(pl.program_id(2) * tq + j * sub, sub)
        o, l = _softmax_pv_t(q_scr[pl.ds(row, sub), :], k_scr, vt_scr)
        o_ref[j * sub:(j + 1) * sub, :] = (o * (1.0 / l)).T.astype(BF16)


def mla_attention(q_mla, kv, z, gq, gk, cos_f, sin_s, n_q, q_blk0, segs, tq):
    n_seg = len(segs)
    seg_rows = tuple(r for r, _ in segs)
    n_k = sum(seg_rows)
    tq = min(tq, n_q)
    nqb = n_q // tq
    ckr = Z_KR // LANE
    vec = pl.BlockSpec((1, 2 * LANE), lambda b, h, qb: (0, 0))
    qtab = pl.BlockSpec((n_q, LANE), lambda b, h, qb: (0, 0))
    ktab = pl.BlockSpec((n_k, LANE), lambda b, h, qb: (0, 0))
    in_specs = ([pl.BlockSpec((n_q, 2 * LANE), lambda b, h, qb: (q_blk0 + b, h))]
                + [pl.BlockSpec((r, 2 * LANE), (lambda off: lambda b, h, qb: (off + b, h))(off)) for r, off in segs]
                + [pl.BlockSpec((r, LANE), (lambda off: lambda b, h, qb: (off + b, ckr))(off)) for r, off in segs]
                + [vec, vec, qtab, qtab, ktab, ktab])
    return pl.pallas_call(
        functools.partial(_mla_attn_kernel, n_seg, seg_rows, tq),
        grid=(BATCH, HEADS, nqb),
        in_specs=in_specs,
        out_specs=pl.BlockSpec((tq, LANE), lambda b, h, qb: (b * nqb + qb, h)),
        out_shape=jax.ShapeDtypeStruct((BATCH * n_q, HEADS * LANE), BF16),
        scratch_shapes=[pltpu.VMEM((n_k, 2 * LANE), BF16), pltpu.VMEM((2 * LANE, n_k), BF16),
                        pltpu.VMEM((n_q, 2 * LANE), BF16)],
        compiler_params=_cp(3), name="mla_attention",
    )(q_mla, *([kv] * n_seg), *([z] * n_seg), gq, gk, cos_f, sin_s, cos_f, sin_s)


def _merge_kernel(yf_ref, od_ref, om_ref, g0_ref, g1_ref, g2_ref, wf_ref, wd_ref, wm_ref, o_ref):
    f = _dot(yf_ref[...], wf_ref[...])
    d = _dot(od_ref[...], wd_ref[...])
    m = _dot(om_ref[...], wm_ref[...])
    o = (_sigmoid(g0_ref[...].astype(F32)) * f + _sigmoid(g1_ref[...].astype(F32)) * d
         + _sigmoid(g2_ref[...].astype(F32)) * m)
    o_ref[...] = o.astype(BF16)


def merge_branches(yf, od, om, z, w_f, w_d, w_m, n_rows):
    tm, tn = TM, 512
    k = FNET_W
    a_spec = pl.BlockSpec((tm, k), lambda i, j: (i, 0))
    w_spec = pl.BlockSpec((k, tn), lambda i, j: (0, j))
    g0 = Z_GATE // tn

    def gspec(r):
        return pl.BlockSpec((tm, tn), lambda i, j: (i, g0 + r * (D // tn) + j))

    return pl.pallas_call(
        _merge_kernel,
        grid=(n_rows // tm, D // tn),
        in_specs=[a_spec, a_spec, a_spec, gspec(0), gspec(1), gspec(2), w_spec, w_spec, w_spec],
        out_specs=pl.BlockSpec((tm, tn), lambda i, j: (i, j)),
        out_shape=jax.ShapeDtypeStruct((n_rows, D), BF16),
        compiler_params=_cp(2), name="merge_branches",
    )(yf, od, om, z, z, z, w_f, w_d, w_m)


def _split3(x):
    hi = x.astype(BF16)
    r = x - hi.astype(F32)
    mid = r.astype(BF16)
    lo = (r - mid.astype(F32)).astype(BF16)
    return hi, mid, lo


HALF = D // 2
HI16 = 0xFFFF0000


def _pack_bf16_pair(x):
    lo = lax.bitcast_convert_type(x[:, :HALF].astype(BF16).astype(F32), jnp.uint32) >> 16
    hi = lax.bitcast_convert_type(x[:, HALF:].astype(BF16).astype(F32), jnp.uint32) & jnp.uint32(HI16)
    return lo | hi


def _unpack_bf16_pair(p):
    lo = lax.bitcast_convert_type(p << 16, F32).astype(BF16)
    hi = lax.bitcast_convert_type(p & jnp.uint32(HI16), F32).astype(BF16)
    return lo, hi


def _router_kernel(x_ref, g_ref, sh_ref, sc_ref, wh_ref, wm_ref, wl_ref, u_ref, rt_ref):
    u = _modulate(x_ref[...], g_ref[...], sh_ref[0], sc_ref[0])
    u_ref[...] = _pack_bf16_pair(u)
    uh, um, ul = _split3(u)
    wh, wm, wl = wh_ref[...], wm_ref[...], wl_ref[...]
    logits = (_dot(ul, wh) + _dot(uh, wl) + _dot(um, wm)) + (_dot(um, wh) + _dot(uh, wm)) + _dot(uh, wh)
    lane = lax.broadcasted_iota(jnp.int32, logits.shape, 1).astype(F32)
    neg = -jnp.inf
    lg = jnp.where(lane < N_EXPERTS, logits, neg)
    m1 = jnp.max(lg, axis=-1, keepdims=True)
    i1 = jnp.min(jnp.where(lg == m1, lane, float(LANE)), axis=-1, keepdims=True)
    lg2 = jnp.where(lane == i1, neg, lg)
    m2 = jnp.max(lg2, axis=-1, keepdims=True)
    i2 = jnp.min(jnp.where(lg2 == m2, lane, float(LANE)), axis=-1, keepdims=True)
    p1 = 1.0 / (1.0 + jnp.exp(m2 - m1))
    p2 = 1.0 - p1
    rt = jnp.where(lane == 0.0, i1, jnp.where(lane == 1.0, i2, jnp.where(lane == 2.0, p1,
                                                                      jnp.where(lane == 3.0, p2, 0.0))))
    rt_ref[...] = rt


def moe_router(x, g, mod, chunk, w_router, n_rows):
    tm = 512
    row = _mod_row(tm)
    wpad = jnp.pad(w_router, ((0, 0), (0, LANE - N_EXPERTS)))
    wh, wm, wl = _split3(wpad)
    wspec = pl.BlockSpec((D, LANE), lambda i: (0, 0))
    return pl.pallas_call(
        _router_kernel,
        grid=(n_rows // tm,),
        in_specs=[pl.BlockSpec((tm, D), lambda i: (i, 0)),
                  pl.BlockSpec((1, D), lambda i: (0, 0)),
                  pl.BlockSpec((1, 1, D), lambda i: (row(i), 0, chunk)),
                  pl.BlockSpec((1, 1, D), lambda i: (row(i), 0, chunk + 1)),
                  wspec, wspec, wspec],
        out_specs=[pl.BlockSpec((tm, HALF), lambda i: (i, 0)), pl.BlockSpec((tm, LANE), lambda i: (i, 0))],
        out_shape=[jax.ShapeDtypeStruct((n_rows, HALF), jnp.uint32),
                   jax.ShapeDtypeStruct((n_rows, LANE), F32)],
        compiler_params=_cp(1), name="moe_router",
    )(x, g, mod, mod, wh, wm, wl)


def _row_copy(src_hbm, dst_ref, sem, src_row, dst_row):
    return pltpu.make_async_copy(src_hbm.at[pl.ds(src_row, 1), :], dst_ref.at[pl.ds(dst_row, 1), :], sem)


def _gather_kernel(idx_ref, src_hbm, o_ref, sem):
    base = pl.program_id(0) * TG

    def issue(r, c):
        _row_copy(src_hbm, o_ref, sem, idx_ref[base + r], r).start()
        return c

    lax.fori_loop(0, TG, issue, 0, unroll=8)

    def wait(r, c):
        _row_copy(src_hbm, o_ref, sem, 0, r).wait()
        return c

    lax.fori_loop(0, TG, wait, 0, unroll=8)


def gather_rows(src, idx):
    n_out = idx.shape[0]
    return pl.pallas_call(
        _gather_kernel,
        grid_spec=pltpu.PrefetchScalarGridSpec(
            num_scalar_prefetch=1, grid=(n_out // TG,),
            in_specs=[pl.BlockSpec(memory_space=pl.ANY)],
            out_specs=pl.BlockSpec((TG, src.shape[1]), lambda i, idx: (i, 0)),
            scratch_shapes=[pltpu.SemaphoreType.DMA]),
        out_shape=jax.ShapeDtypeStruct((n_out, src.shape[1]), src.dtype),
        compiler_params=_cp(1), name="gather_rows",
    )(idx, src)


def _moe_kernel(te_ref, nu_ref, xs_ref, w1_ref, w3_ref, w2_ref, o_ref, xb_scr, acc_scr):
    i, j = pl.program_id(0), pl.program_id(1)
    last = pl.num_programs(1) - 1
    used = i < nu_ref[0]

    @pl.when(used)
    def _():
        @pl.when(j == 0)
        def _():
            lo, hi = _unpack_bf16_pair(xs_ref[...])
            xb_scr[:, :HALF] = lo
            xb_scr[:, HALF:] = hi
            acc_scr[...] = jnp.zeros_like(acc_scr)

        xb = xb_scr[...]
        a = _dot(xb, w1_ref[0])
        b = _dot(xb, w3_ref[0])
        h = (a * _sigmoid(a) * b).astype(BF16)
        acc_scr[...] += _dot(h, w2_ref[0])

        @pl.when(j == last)
        def _():
            o_ref[...] = acc_scr[...]

    @pl.when(jnp.logical_and(jnp.logical_not(used), j == last))
    def _():
        o_ref[...] = jnp.zeros_like(o_ref)


def moe_experts(xs, tile_expert, n_used, w1, w3, w2):
    p = xs.shape[0]
    tm, tf = TM_MOE, TF_MOE
    return pl.pallas_call(
        _moe_kernel,
        grid_spec=pltpu.PrefetchScalarGridSpec(
            num_scalar_prefetch=2, grid=(p // tm, D_FF_EXPERT // tf),
            in_specs=[pl.BlockSpec((tm, HALF), lambda i, j, te, nu: (i, 0)),
                      pl.BlockSpec((1, D, tf), lambda i, j, te, nu: (te[i], 0, j)),
                      pl.BlockSpec((1, D, tf), lambda i, j, te, nu: (te[i], 0, j)),
                      pl.BlockSpec((1, tf, D), lambda i, j, te, nu: (te[i], j, 0))],
            out_specs=pl.BlockSpec((tm, D), lambda i, j, te, nu: (i, 0)),
            scratch_shapes=[pltpu.VMEM((tm, D), BF16), pltpu.VMEM((tm, D), F32)]),
        out_shape=jax.ShapeDtypeStruct((p, D), F32),
        compiler_params=_cp(2), name="moe_experts",
    )(tile_expert, n_used, xs, w1, w3, w2)


def _combine_kernel(n_tok, pos_ref, ys_hbm, x_ref, rt_ref, gate_ref, o_ref, buf, sem):
    base = pl.program_id(0) * TG

    def issue(r, c):
        _row_copy(ys_hbm, buf.at[0], sem, pos_ref[base + r], r).start()
        _row_copy(ys_hbm, buf.at[1], sem, pos_ref[n_tok + base + r], r).start()
        return c

    lax.fori_loop(0, TG, issue, 0, unroll=8)

    def wait(r, c):
        _row_copy(ys_hbm, buf.at[0], sem, 0, r).wait()
        _row_copy(ys_hbm, buf.at[1], sem, 0, r).wait()
        return c

    lax.fori_loop(0, TG, wait, 0, unroll=8)
    rt = rt_ref[...]
    y = rt[:, 2:3] * buf[0] + rt[:, 3:4] * buf[1]
    o_ref[...] = x_ref[...] + gate_ref[0] * y


def moe_combine(ys, pos, x, rt, mod, chunk, n_rows):
    row = _mod_row(TG)
    return pl.pallas_call(
        functools.partial(_combine_kernel, n_rows),
        grid_spec=pltpu.PrefetchScalarGridSpec(
            num_scalar_prefetch=1, grid=(n_rows // TG,),
            in_specs=[pl.BlockSpec(memory_space=pl.ANY),
                      pl.BlockSpec((TG, D), lambda i, pos: (i, 0)),
                      pl.BlockSpec((TG, LANE), lambda i, pos: (i, 0)),
                      pl.BlockSpec((1, 1, D), lambda i, pos: (row(i), 0, chunk))],
            out_specs=pl.BlockSpec((TG, D), lambda i, pos: (i, 0)),
            scratch_shapes=[pltpu.VMEM((2, TG, D), F32), pltpu.SemaphoreType.DMA]),
        out_shape=jax.ShapeDtypeStruct((n_rows, D), F32),
        compiler_params=_cp(1), name="moe_combine",
    )(pos, ys, x, rt, mod)


def moe_ffn(x, g, mod, w_router, w1, w3, w2, n_rows):
    u, rt = moe_router(x, g, mod, 3, w_router, n_rows)
    tm = TM_MOE
    n_pairs = 2 * n_rows
    p_rows = n_pairs + N_EXPERTS * tm
    n_tiles = p_rows // tm
    e_flat = jnp.concatenate([rt[:, 0], rt[:, 1]]).astype(jnp.int32)
    onehot = (e_flat[:, None] == jnp.arange(N_EXPERTS, dtype=jnp.int32)[None, :]).astype(jnp.int32)
    csum = jnp.cumsum(onehot, axis=0)
    counts = csum[-1]
    rank = jnp.sum((csum - onehot) * onehot, axis=1)
    padded = ((counts + tm - 1) // tm) * tm
    g_end = jnp.cumsum(padded)
    g_start = g_end - padded
    dest = g_start[e_flat] + rank
    tok = jnp.tile(jnp.arange(n_rows, dtype=jnp.int32), 2)
    src = jnp.zeros((p_rows,), jnp.int32).at[dest].set(tok)
    tile_start = jnp.arange(n_tiles, dtype=jnp.int32) * tm
    tile_expert = jnp.minimum(jnp.sum((tile_start[:, None] >= g_end[None, :]).astype(jnp.int32), axis=1),
                              N_EXPERTS - 1).astype(jnp.int32)
    n_used = (g_end[-1] // tm).astype(jnp.int32).reshape(1)
    xs = gather_rows(u, src)
    ys = moe_experts(xs, tile_expert, n_used, w1, w3, w2)
    return moe_combine(ys, dest, x, rt, mod, 5, n_rows)


def kernel(x, c, ctx, c_ctx, w_mod, b_mod, g_norm1, g_norm2, w_in, g_diff_q, g_diff_k, diff_lambda,
           g_diff_out, w_diff_o, g_mla_cq, g_mla_ckv, w_mla_uq, w_mla_ukv, g_mla_q, g_mla_k, w_mla_o,
           w_fnet, w_out, ffn_w1, ffn_w3, ffn_w2, moe_router, moe_w1, moe_w3, moe_w2):
    xs = jnp.concatenate([x.reshape(N_LAT, D), ctx.reshape(N_CTX, D)], axis=0)
    c_all = jnp.zeros((32, D), F32).at[:BATCH].set(c).at[BATCH].set(c_ctx)
    mods = mod_tables(c_all, w_mod, b_mod)

    cos_lat, sin_lat = rope_tables(SEQ, CTX)
    cos_ctx, sin_ctx = rope_tables(0, CTX)
    cw, sw = dft_matrices(FNET_GROUP_W, FNET_GROUP_W ** -0.5)
    cn_lat, sn_lat = dft_matrices(SEQ, SEQ ** -0.5)
    cn_ctx, sn_ctx = dft_matrices(CTX, CTX ** -0.5)
    sn_lat, sn_ctx = -sn_lat, -sn_ctx

    lat_segs = [(SEQ, 0), (CTX, N_LAT // CTX)]
    ctx_segs = [(CTX, N_LAT // CTX)]

    for i in range(DEPTH):
        need_ctx = i < DEPTH - 1
        n_rows = N_TOK if need_ctx else N_LAT
        mod = mods[i].reshape(32, 1, N_MOD * D)
        wi = w_in[i]
        w_in_p = jnp.concatenate(
            [wi[:, :Z_KR], wi[:, Z_KR:Z_KR + MLA_ROPE], jnp.zeros((D, Z_GATE - Z_KR - MLA_ROPE), F32),
             wi[:, Z_KR + MLA_ROPE:]], axis=1).astype(BF16)
        w_uq = jnp.pad(w_mla_uq[i].reshape(MLA_Q_LORA, HEADS, MLA_QK),
                       ((0, 0), (0, 0), (0, 256 - MLA_QK))).reshape(MLA_Q_LORA, HEADS * 256).astype(BF16)
        w_ukv = w_mla_ukv[i].astype(BF16)
        g_q = jnp.pad(g_mla_q[i], (0, 256 - MLA_QK)).reshape(1, 256)
        g_k = jnp.pad(g_mla_k[i], (0, 256 - MLA_QK)).reshape(1, 256)
        lam_init = 0.8 - 0.6 * math.exp(-0.3 * i)
        lp = diff_lambda[i]
        lam = (jnp.exp(jnp.sum(lp[0] * lp[1])) - jnp.exp(jnp.sum(lp[2] * lp[3])) + lam_init).reshape(1)
        gdq = jnp.tile(g_diff_q[i], 2).reshape(1, LANE)
        gdk = jnp.tile(g_diff_k[i], 2).reshape(1, LANE)
        gdo = (g_diff_out[i] * (1.0 - lam_init)).reshape(1, LANE)

        z = norm_matmul(xs, g_norm1[i].reshape(1, D), mod, 0, [w_in_p], 1024, N_TOK)
        zc, zs = dft_channels(z, cw, sw)
        yf = dft_positions(cn_lat, sn_lat, zc, zs, SEQ, 0, 512)
        q_mla, kv = mla_up(z, g_mla_cq[i].reshape(1, -1), g_mla_ckv[i].reshape(1, -1), w_uq, w_ukv)
        od = diff_attention(z, lam, gdq, gdk, gdo, cos_lat, sin_lat, SEQ, 0, lat_segs, TQ_DIFF)
        om = mla_attention(q_mla, kv, z, g_q, g_k, cos_lat, sin_lat, SEQ, 0, lat_segs, TQ_MLA)
        if need_ctx:
            yf = jnp.concatenate([yf, dft_positions(cn_ctx, sn_ctx, zc, zs, CTX, N_LAT // CTX, CTX)], axis=0)
            od = jnp.concatenate(
                [od, diff_attention(z, lam, gdq, gdk, gdo, cos_ctx, sin_ctx, CTX, N_LAT // CTX, ctx_segs, CTX)],
                axis=0)
            om = jnp.concatenate(
                [om, mla_attention(q_mla, kv, z, g_q, g_k, cos_ctx, sin_ctx, CTX, N_LAT // CTX, ctx_segs, CTX)],
                axis=0)
        merged = merge_branches(yf, od, om, z, w_fnet[i].astype(BF16), w_diff_o[i].astype(BF16),
                                w_mla_o[i].astype(BF16), n_rows)
        xs = matmul_residual(merged, w_out[i].astype(BF16), xs, mod, 2, TM, 1024, n_rows)

        g2 = g_norm2[i].reshape(1, D)
        if i % 2 == 0:
            j = i // 2
            h = norm_matmul(xs, g2, mod, 3, [ffn_w1[j].astype(BF16), ffn_w3[j].astype(BF16)], 512, n_rows)
            xs = matmul_residual(h, ffn_w2[j].astype(BF16), xs, mod, 5, TM_FFN2, 512, n_rows)
        else:
            j = i // 2
            xs = moe_ffn(xs, g2, mod, moe_router[j], moe_w1[j].astype(BF16), moe_w3[j].astype(BF16),
                         moe_w2[j].astype(BF16), n_rows)
    return xs[:N_LAT].reshape(BATCH, SEQ, D)
```

```python
import functools
import math

import jax
import jax.numpy as jnp
from jax import lax
from jax.experimental import pallas as pl
from jax.experimental.pallas import tpu as pltpu

F32 = jnp.float32
BF16 = jnp.bfloat16

D = 2048
BATCH = 16
SEQ = 2048
DEPTH = 4
GRID_W = 64
CTX = 256
N_LAT = BATCH * SEQ
N_CTX = BATCH * CTX
N_TOK = N_LAT + N_CTX
HEADS = 8
FNET_W = 1024
FNET_GROUP_W = 256
MLA_Q_LORA = 512
MLA_KV_LORA = 256
MLA_NOPE = 128
MLA_ROPE = 64
MLA_QK = MLA_NOPE + MLA_ROPE
D_FF_DENSE = 5632
N_EXPERTS = 8
D_FF_EXPERT = 4096
ROPE_THETA = 10000.0
EPS = 1e-6
N_MOD = 6
LOG2E = math.log2(math.e)

Z_F, Z_DQ, Z_DK, Z_DV, Z_CQ, Z_CKV, Z_KR, Z_GATE = 0, 1024, 2048, 3072, 4096, 4608, 4864, 5120
Z_COLS = Z_GATE + 3 * D

LANE = 128
VMEM_LIMIT = 56 * 1024 * 1024

TM = 1024
TM_FFN2 = 512
TQ_DIFF, SUB_DIFF = 2048, 256
TQ_MLA, SUB_MLA = 2048, 512
TM_MOE = 512
TF_MOE = 512
TG = 256
DMA_GROUP = 8


def _cp(n_axes):
    return pltpu.CompilerParams(dimension_semantics=("arbitrary",) * n_axes,
                                vmem_limit_bytes=VMEM_LIMIT)


def _dot(a, b):
    return jnp.dot(a, b, preferred_element_type=F32)


def _dot_nt(a, b):
    return lax.dot_general(a, b, (((1,), (1,)), ((), ())), preferred_element_type=F32)


def _sigmoid(x):
    return 1.0 / (1.0 + jnp.exp(-x))


def _mod_row(tm):
    return lambda i: jnp.minimum((i * tm) // SEQ, BATCH)


def _mod_kernel(c_ref, w_ref, b_ref, o_ref):
    c = c_ref[...]
    a = (c * _sigmoid(c)).astype(BF16)
    o_ref[0] = _dot(a, w_ref[0].astype(BF16)) + b_ref[0]


def mod_tables(c_all, w_mod, b_mod):
    tn = 1024
    rows = c_all.shape[0]
    return pl.pallas_call(
        _mod_kernel,
        grid=(DEPTH, N_MOD * D // tn),
        in_specs=[pl.BlockSpec((rows, D), lambda l, j: (0, 0)),
                  pl.BlockSpec((1, D, tn), lambda l, j: (l, 0, j)),
                  pl.BlockSpec((1, 1, tn), lambda l, j: (l, 0, j))],
        out_specs=pl.BlockSpec((1, rows, tn), lambda l, j: (l, 0, j)),
        out_shape=jax.ShapeDtypeStruct((DEPTH, rows, N_MOD * D), F32),
        compiler_params=_cp(2), name="mod_tables",
    )(c_all, w_mod, b_mod.reshape(DEPTH, 1, N_MOD * D))


def _modulate(x, g, shift, scale):
    ms = jnp.mean(x * x, axis=-1, keepdims=True)
    y = x * lax.rsqrt(ms + EPS) * g
    return y * (1.0 + scale) + shift


def _norm_mm_kernel(x_ref, g_ref, sh_ref, sc_ref, w_ref, o_ref, u_scr):
    @pl.when(pl.program_id(1) == 0)
    def _():
        u_scr[...] = _modulate(x_ref[...], g_ref[...], sh_ref[0], sc_ref[0]).astype(BF16)

    o_ref[...] = _dot(u_scr[...], w_ref[...]).astype(o_ref.dtype)


def _norm_swiglu_kernel(x_ref, g_ref, sh_ref, sc_ref, w1_ref, w3_ref, o_ref, u_scr):
    @pl.when(pl.program_id(1) == 0)
    def _():
        u_scr[...] = _modulate(x_ref[...], g_ref[...], sh_ref[0], sc_ref[0]).astype(BF16)

    u = u_scr[...]
    a = _dot(u, w1_ref[...])
    b = _dot(u, w3_ref[...])
    o_ref[...] = (a * _sigmoid(a) * b).astype(o_ref.dtype)


def norm_matmul(x, g, mod, chunk, ws, tn, n_rows):
    n = ws[0].shape[1]
    tm = TM
    kern = _norm_mm_kernel if len(ws) == 1 else _norm_swiglu_kernel
    row = _mod_row(tm)
    return pl.pallas_call(
        kern,
        grid=(n_rows // tm, n // tn),
        in_specs=[pl.BlockSpec((tm, D), lambda i, j: (i, 0)),
                  pl.BlockSpec((1, D), lambda i, j: (0, 0)),
                  pl.BlockSpec((1, 1, D), lambda i, j: (row(i), 0, chunk)),
                  pl.BlockSpec((1, 1, D), lambda i, j: (row(i), 0, chunk + 1))]
                 + [pl.BlockSpec((D, tn), lambda i, j: (0, j)) for _ in ws],
        out_specs=pl.BlockSpec((tm, tn), lambda i, j: (i, j)),
        out_shape=jax.ShapeDtypeStruct((n_rows, n), BF16),
        scratch_shapes=[pltpu.VMEM((tm, D), BF16)],
        compiler_params=_cp(2), name="norm_matmul%d" % len(ws),
    )(x, g, mod, mod, *ws)


def _mm_res_kernel(a_ref, w_ref, x_ref, gate_ref, o_ref):
    o_ref[...] = x_ref[...] + gate_ref[0] * _dot(a_ref[...], w_ref[...])


def matmul_residual(a, w, x, mod, chunk, tm, tn, n_rows):
    k = a.shape[1]
    row = _mod_row(tm)
    nj = D // tn
    return pl.pallas_call(
        _mm_res_kernel,
        grid=(n_rows // tm, nj),
        in_specs=[pl.BlockSpec((tm, k), lambda i, j: (i, 0)),
                  pl.BlockSpec((k, tn), lambda i, j: (0, j)),
                  pl.BlockSpec((tm, tn), lambda i, j: (i, j)),
                  pl.BlockSpec((1, 1, tn), lambda i, j: (row(i), 0, chunk * nj + j))],
        out_specs=pl.BlockSpec((tm, tn), lambda i, j: (i, j)),
        out_shape=jax.ShapeDtypeStruct((n_rows, D), F32),
        compiler_params=_cp(2), name="matmul_residual",
    )(a, w, x, mod)


def _dft_kernel(cn_ref, sn_ref, z_ref, cw_ref, sw_ref, y_ref, zc_scr, zs_scr):
    @pl.when(pl.program_id(1) == 0)
    def _():
        gw = FNET_GROUP_W
        for g in range(FNET_W // gw):
            zg = z_ref[:, g * gw:(g + 1) * gw]
            zc_scr[:, g * gw:(g + 1) * gw] = _dot(zg, cw_ref[...]).astype(BF16)
            zs_scr[:, g * gw:(g + 1) * gw] = _dot(zg, sw_ref[...]).astype(BF16)

    y_ref[...] = (_dot(cn_ref[...], zc_scr[...]) + _dot(sn_ref[...], zs_scr[...])).astype(BF16)


def fourier_mix(cn, sn, z, cw, sw, n, row_blk0, tm):
    mt = n // tm
    gw = FNET_GROUP_W
    mspec = pl.BlockSpec((tm, n), lambda b, m: (m, 0))
    wspec = pl.BlockSpec((gw, gw), lambda b, m: (0, 0))
    return pl.pallas_call(
        _dft_kernel,
        grid=(BATCH, mt),
        in_specs=[mspec, mspec, pl.BlockSpec((n, FNET_W), lambda b, m: (row_blk0 + b, 0)), wspec, wspec],
        out_specs=pl.BlockSpec((tm, FNET_W), lambda b, m: (b * mt + m, 0)),
        out_shape=jax.ShapeDtypeStruct((BATCH * n, FNET_W), BF16),
        scratch_shapes=[pltpu.VMEM((n, FNET_W), BF16), pltpu.VMEM((n, FNET_W), BF16)],
        compiler_params=_cp(2), name="fourier_mix",
    )(cn, sn, z, cw, sw)


def dft_matrices(n, scale):
    idx = jnp.arange(n, dtype=jnp.int32)
    prod = (idx[:, None] * idx[None, :]) % n
    ang = prod.astype(F32) * (2.0 * math.pi / n)
    return (jnp.cos(ang) * scale).astype(BF16), (jnp.sin(ang) * scale).astype(BF16)


def rope_tables(n_lat, n_ctx):
    n_freq = 16
    inv_freq = ROPE_THETA ** (-jnp.arange(n_freq, dtype=F32) / n_freq)
    cos_parts, sin_parts = [], []
    if n_lat:
        rows = n_lat // GRID_W
        row = jnp.repeat(jnp.arange(rows, dtype=F32), GRID_W)
        col = jnp.tile(jnp.arange(GRID_W, dtype=F32), rows)
        ang = jnp.concatenate([row[:, None] * inv_freq, col[:, None] * inv_freq], axis=-1)
        cos_parts.append(jnp.cos(ang))
        sin_parts.append(jnp.sin(ang))
    if n_ctx:
        cos_parts.append(jnp.ones((n_ctx, 32), F32))
        sin_parts.append(jnp.zeros((n_ctx, 32), F32))
    cos = jnp.concatenate(cos_parts, axis=0)
    sin = jnp.concatenate(sin_parts, axis=0)
    cos_f = jnp.tile(jnp.concatenate([cos, cos], axis=-1), (1, 2))
    sin_s = jnp.tile(jnp.concatenate([-sin, sin], axis=-1), (1, 2))
    return cos_f, sin_s


def _rope(x, cos_f, sin_s, first_half):
    swapped = jnp.where(first_half, pltpu.roll(x, 96, 1), pltpu.roll(x, 32, 1))
    return x * cos_f + swapped * sin_s


def _softmax_pv(q, k_scr, v_scr):
    s = _dot_nt(q, k_scr[...]).astype(BF16)
    m = jnp.max(s, axis=-1, keepdims=True)
    acc = _dot(jnp.exp2(s - m), v_scr[...])
    return acc[:, :LANE], acc[:, LANE:]


def _softmax_pv_t(q, k_scr, vt_scr):
    s = _dot_nt(k_scr[...], q).astype(BF16)
    m = jnp.max(s, axis=0, keepdims=True)
    acc = _dot(vt_scr[...], jnp.exp2(s - m))
    return acc[:LANE, :], acc[LANE:, :]


def _diff_attn_kernel(n_seg, seg_rows, tq, *refs):
    lam_ref, q_ref = refs[0], refs[1]
    k_refs = refs[2:2 + n_seg]
    v_refs = refs[2 + n_seg:2 + 2 * n_seg]
    (gq_ref, gk_ref, go_ref, cq_ref, sq_ref, ck_ref, sk_ref, o_ref,
     k_scr, v_scr, q0_scr, q1_scr) = refs[2 + 2 * n_seg:]

    lane = lax.broadcasted_iota(jnp.int32, (1, LANE), 1)
    first_half = (lane % 64) < 32
    comp0 = lane < 64

    def head_norm(x, g):
        gi = lax.broadcasted_iota(jnp.int32, (LANE, LANE), 0) // 64
        gj = lax.broadcasted_iota(jnp.int32, (LANE, LANE), 1) // 64
        group_mean = jnp.where(gi == gj, 1.0 / 64.0, 0.0).astype(BF16)
        ms = _dot((x * x).astype(BF16), group_mean)
        return x * lax.rsqrt(ms + EPS) * g

    @pl.when(pl.program_id(2) == 0)
    def _():
        r0 = 0
        for s in range(n_seg):
            n = seg_rows[s]
            k = head_norm(k_refs[s][...].astype(F32), gk_ref[...])
            k = _rope(k, ck_ref[r0:r0 + n, :], sk_ref[r0:r0 + n, :], first_half)
            k_scr[r0:r0 + n, :] = k.astype(BF16)
            v_scr[r0:r0 + n, :LANE] = v_refs[s][...]
            v_scr[r0:r0 + n, LANE:] = jnp.ones((n, LANE), BF16)
            r0 += n
        q = head_norm(q_ref[...].astype(F32), gq_ref[...])
        q = _rope(q, cq_ref[...], sq_ref[...], first_half) * (64.0 ** -0.5 * LOG2E)
        q0_scr[...] = jnp.where(comp0, q, 0.0).astype(BF16)
        q1_scr[...] = jnp.where(comp0, 0.0, q).astype(BF16)

    sub = min(tq, SUB_DIFF)
    for j in range(tq // sub):
        row = pl.multiple_of(pl.program_id(2) * tq + j * sub, sub)
        o0, l0 = _softmax_pv(q0_scr[pl.ds(row, sub), :], k_scr, v_scr)
        o1, l1 = _softmax_pv(q1_scr[pl.ds(row, sub), :], k_scr, v_scr)
        o = o0 * (1.0 / l0) - o1 * (lam_ref[0] / l1)
        ms = jnp.mean(o * o, axis=-1, keepdims=True)
        o_ref[j * sub:(j + 1) * sub, :] = (o * lax.rsqrt(ms + EPS) * go_ref[...]).astype(BF16)


def diff_attention(z, lam, gq, gk, go, cos_f, sin_s, n_q, q_blk0, segs, tq):
    n_seg = len(segs)
    seg_rows = tuple(r for r, _ in segs)
    n_k = sum(seg_rows)
    tq = min(tq, n_q)
    nqb = n_q // tq
    hq, hk, hv = Z_DQ // LANE, Z_DK // LANE, Z_DV // LANE

    def kspec(rows, off, col0):
        return pl.BlockSpec((rows, LANE), lambda b, h, qb: (off + b, col0 + h))

    vec = pl.BlockSpec((1, LANE), lambda b, h, qb: (0, 0))
    qtab = pl.BlockSpec((n_q, LANE), lambda b, h, qb: (0, 0))
    ktab = pl.BlockSpec((n_k, LANE), lambda b, h, qb: (0, 0))
    in_specs = ([pl.BlockSpec(memory_space=pltpu.SMEM), kspec(n_q, q_blk0, hq)]
                + [kspec(r, off, hk) for r, off in segs]
                + [kspec(r, off, hv) for r, off in segs]
                + [vec, vec, vec, qtab, qtab, ktab, ktab])
    return pl.pallas_call(
        functools.partial(_diff_attn_kernel, n_seg, seg_rows, tq),
        grid=(BATCH, HEADS, nqb),
        in_specs=in_specs,
        out_specs=pl.BlockSpec((tq, LANE), lambda b, h, qb: (b * nqb + qb, h)),
        out_shape=jax.ShapeDtypeStruct((BATCH * n_q, HEADS * LANE), BF16),
        scratch_shapes=[pltpu.VMEM((n_k, LANE), BF16), pltpu.VMEM((n_k, 2 * LANE), BF16),
                        pltpu.VMEM((n_q, LANE), BF16), pltpu.VMEM((n_q, LANE), BF16)],
        compiler_params=_cp(3), name="diff_attention",
    )(lam, z, *([z] * (2 * n_seg)), gq, gk, go, cos_f, sin_s, cos_f, sin_s)


def _mla_up_kernel(cq_ref, ckv_ref, gq_ref, gkv_ref, wq_ref, wkv_ref, q_ref, kv_ref):
    def norm(x, g):
        ms = jnp.mean(x * x, axis=-1, keepdims=True)
        return (x * lax.rsqrt(ms + EPS) * g).astype(BF16)

    q_ref[...] = _dot(norm(cq_ref[...].astype(F32), gq_ref[...]), wq_ref[...]).astype(BF16)
    kv_ref[...] = _dot(norm(ckv_ref[...].astype(F32), gkv_ref[...]), wkv_ref[...]).astype(BF16)


def mla_up(z, g_cq, g_ckv, w_uq, w_ukv):
    tm = 1024
    n = HEADS * 256
    return pl.pallas_call(
        _mla_up_kernel,
        grid=(N_TOK // tm,),
        in_specs=[pl.BlockSpec((tm, MLA_Q_LORA), lambda i: (i, Z_CQ // MLA_Q_LORA)),
                  pl.BlockSpec((tm, MLA_KV_LORA), lambda i: (i, Z_CKV // MLA_KV_LORA)),
                  pl.BlockSpec((1, MLA_Q_LORA), lambda i: (0, 0)),
                  pl.BlockSpec((1, MLA_KV_LORA), lambda i: (0, 0)),
                  pl.BlockSpec((MLA_Q_LORA, n), lambda i: (0, 0)),
                  pl.BlockSpec((MLA_KV_LORA, n), lambda i: (0, 0))],
        out_specs=[pl.BlockSpec((tm, n), lambda i: (i, 0))] * 2,
        out_shape=[jax.ShapeDtypeStruct((N_TOK, n), BF16)] * 2,
        compiler_params=_cp(1), name="mla_up",
    )(z, z, g_cq, g_ckv, w_uq, w_ukv)


def _mla_attn_kernel(n_seg, seg_rows, tq, *refs):
    q_ref = refs[0]
    kv_refs = refs[1:1 + n_seg]
    kr_refs = refs[1 + n_seg:1 + 2 * n_seg]
    gq_ref, gk_ref, cq_ref, sq_ref, ck_ref, sk_ref, o_ref, k_scr, vt_scr, q_scr = refs[1 + 2 * n_seg:]

    lane = lax.broadcasted_iota(jnp.int32, (1, LANE), 1)
    first_half = (lane % 64) < 32

    def qk_norm(nope, rope, g_ref, scale):
        ones = jnp.ones((LANE, LANE), BF16)
        ss = _dot((nope * nope).astype(BF16), ones) + _dot((rope * rope).astype(BF16), ones)
        inv = lax.rsqrt(ss * (1.0 / MLA_QK) + EPS) * scale
        return ((nope * inv * g_ref[:, :LANE]).astype(BF16), (rope * inv * g_ref[:, LANE:]).astype(BF16))

    @pl.when(pl.program_id(2) == 0)
    def _():
        r0 = 0
        for s in range(n_seg):
            n = seg_rows[s]
            kv = kv_refs[s][...]
            rope = _rope(kr_refs[s][...].astype(F32), ck_ref[r0:r0 + n, :], sk_ref[r0:r0 + n, :], first_half)
            kn, kr = qk_norm(kv[:, :LANE].astype(F32), rope, gk_ref, 1.0)
            k_scr[r0:r0 + n, :LANE] = kn
            k_scr[r0:r0 + n, LANE:] = kr
            vt_scr[:LANE, r0:r0 + n] = kv[:, LANE:].astype(F32).T.astype(BF16)
            vt_scr[LANE:, r0:r0 + n] = jnp.ones((LANE, n), BF16)
            r0 += n
        q = q_ref[...]
        rope = _rope(q[:, LANE:].astype(F32), cq_ref[...], sq_ref[...], first_half)
        qn, qr = qk_norm(q[:, :LANE].astype(F32), rope, gq_ref, MLA_QK ** -0.5 * LOG2E)
        q_scr[:, :LANE] = qn
        q_scr[:, LANE:] = qr

    sub = min(tq, SUB_MLA)
    for j in range(tq // sub):
        row = pl.multiple_of(pl.program_id(2) * tq + j * sub, sub)
        o, l = _softmax_pv_t(q_scr[pl.ds(row, sub), :], k_scr, vt_scr)
        o_ref[j * sub:(j + 1) * sub, :] = (o * (1.0 / l)).T.astype(BF16)


def mla_attention(q_mla, kv, z, gq, gk, cos_f, sin_s, n_q, q_blk0, segs, tq):
    n_seg = len(segs)
    seg_rows = tuple(r for r, _ in segs)
    n_k = sum(seg_rows)
    tq = min(tq, n_q)
    nqb = n_q // tq
    ckr = Z_KR // LANE
    vec = pl.BlockSpec((1, 2 * LANE), lambda b, h, qb: (0, 0))
    qtab = pl.BlockSpec((n_q, LANE), lambda b, h, qb: (0, 0))
    ktab = pl.BlockSpec((n_k, LANE), lambda b, h, qb: (0, 0))
    in_specs = ([pl.BlockSpec((n_q, 2 * LANE), lambda b, h, qb: (q_blk0 + b, h))]
                + [pl.BlockSpec((r, 2 * LANE), (lambda off: lambda b, h, qb: (off + b, h))(off)) for r, off in segs]
                + [pl.BlockSpec((r, LANE), (lambda off: lambda b, h, qb: (off + b, ckr))(off)) for r, off in segs]
                + [vec, vec, qtab, qtab, ktab, ktab])
    return pl.pallas_call(
        functools.partial(_mla_attn_kernel, n_seg, seg_rows, tq),
        grid=(BATCH, HEADS, nqb),
        in_specs=in_specs,
        out_specs=pl.BlockSpec((tq, LANE), lambda b, h, qb: (b * nqb + qb, h)),
        out_shape=jax.ShapeDtypeStruct((BATCH * n_q, HEADS * LANE), BF16),
        scratch_shapes=[pltpu.VMEM((n_k, 2 * LANE), BF16), pltpu.VMEM((2 * LANE, n_k), BF16),
                        pltpu.VMEM((n_q, 2 * LANE), BF16)],
        compiler_params=_cp(3), name="mla_attention",
    )(q_mla, *([kv] * n_seg), *([z] * n_seg), gq, gk, cos_f, sin_s, cos_f, sin_s)


def _merge_kernel(n_lat_tiles, n_src, *refs):
    srcs = [refs[3 * s:3 * s + 3] for s in range(n_src)]
    g0_ref, g1_ref, g2_ref, wf_ref, wd_ref, wm_ref, o_ref = refs[3 * n_src:]

    def body(yf_ref, od_ref, om_ref):
        f = _dot(yf_ref[...], wf_ref[...])
        d = _dot(od_ref[...], wd_ref[...])
        m = _dot(om_ref[...], wm_ref[...])
        o = (_sigmoid(g0_ref[...].astype(F32)) * f + _sigmoid(g1_ref[...].astype(F32)) * d
             + _sigmoid(g2_ref[...].astype(F32)) * m)
        o_ref[...] = o.astype(BF16)

    if n_src == 1:
        body(*srcs[0])
    else:
        i = pl.program_id(0)
        pl.when(i < n_lat_tiles)(lambda: body(*srcs[0]))
        pl.when(i >= n_lat_tiles)(lambda: body(*srcs[1]))


def merge_branches(lat, ctx, z, w_f, w_d, w_m):
    tm, tn = TM, 512
    k = FNET_W
    nl = N_LAT // tm
    n_rows = N_LAT + (N_CTX if ctx is not None else 0)
    w_spec = pl.BlockSpec((k, tn), lambda i, j: (0, j))
    g0 = Z_GATE // tn

    def gspec(r):
        return pl.BlockSpec((tm, tn), lambda i, j: (i, g0 + r * (D // tn) + j))

    lat_spec = pl.BlockSpec((tm, k), lambda i, j: (jnp.minimum(i, nl - 1), 0))
    ctx_spec = pl.BlockSpec((tm, k), lambda i, j: (jnp.maximum(i - nl, 0), 0))
    srcs = list(lat) + (list(ctx) if ctx is not None else [])
    specs = [lat_spec] * 3 + ([ctx_spec] * 3 if ctx is not None else [])
    return pl.pallas_call(
        functools.partial(_merge_kernel, nl, 2 if ctx is not None else 1),
        grid=(n_rows // tm, D // tn),
        in_specs=specs + [gspec(0), gspec(1), gspec(2), w_spec, w_spec, w_spec],
        out_specs=pl.BlockSpec((tm, tn), lambda i, j: (i, j)),
        out_shape=jax.ShapeDtypeStruct((n_rows, D), BF16),
        compiler_params=_cp(2), name="merge_branches",
    )(*srcs, z, z, z, w_f, w_d, w_m)


def _split3(x):
    hi = x.astype(BF16)
    r = x - hi.astype(F32)
    mid = r.astype(BF16)
    lo = (r - mid.astype(F32)).astype(BF16)
    return hi, mid, lo


HALF = D // 2
HI16 = 0xFFFF0000


def _pack_bf16_pair(x):
    lo = lax.bitcast_convert_type(x[:, :HALF].astype(BF16).astype(F32), jnp.uint32) >> 16
    hi = lax.bitcast_convert_type(x[:, HALF:].astype(BF16).astype(F32), jnp.uint32) & jnp.uint32(HI16)
    return lo | hi


def _unpack_bf16_pair(p):
    lo = lax.bitcast_convert_type(p << 16, F32).astype(BF16)
    hi = lax.bitcast_convert_type(p & jnp.uint32(HI16), F32).astype(BF16)
    return lo, hi


def _router_kernel(x_ref, g_ref, sh_ref, sc_ref, wh_ref, wm_ref, wl_ref, u_ref, rt_ref):
    u = _modulate(x_ref[...], g_ref[...], sh_ref[0], sc_ref[0])
    u_ref[...] = _pack_bf16_pair(u)
    uh, um, ul = _split3(u)
    wh, wm, wl = wh_ref[...], wm_ref[...], wl_ref[...]
    logits = (_dot(ul, wh) + _dot(uh, wl) + _dot(um, wm)) + (_dot(um, wh) + _dot(uh, wm)) + _dot(uh, wh)
    lane = lax.broadcasted_iota(jnp.int32, logits.shape, 1).astype(F32)
    neg = -jnp.inf
    lg = jnp.where(lane < N_EXPERTS, logits, neg)
    m1 = jnp.max(lg, axis=-1, keepdims=True)
    i1 = jnp.min(jnp.where(lg == m1, lane, float(LANE)), axis=-1, keepdims=True)
    lg2 = jnp.where(lane == i1, neg, lg)
    m2 = jnp.max(lg2, axis=-1, keepdims=True)
    i2 = jnp.min(jnp.where(lg2 == m2, lane, float(LANE)), axis=-1, keepdims=True)
    p1 = 1.0 / (1.0 + jnp.exp(m2 - m1))
    p2 = 1.0 - p1
    rt = jnp.where(lane == 0.0, i1, jnp.where(lane == 1.0, i2, jnp.where(lane == 2.0, p1,
                                                                      jnp.where(lane == 3.0, p2, 0.0))))
    rt_ref[...] = rt


def moe_router(x, g, mod, chunk, w_router, n_rows):
    tm = 512
    row = _mod_row(tm)
    wpad = jnp.pad(w_router, ((0, 0), (0, LANE - N_EXPERTS)))
    wh, wm, wl = _split3(wpad)
    wspec = pl.BlockSpec((D, LANE), lambda i: (0, 0))
    return pl.pallas_call(
        _router_kernel,
        grid=(n_rows // tm,),
        in_specs=[pl.BlockSpec((tm, D), lambda i: (i, 0)),
                  pl.BlockSpec((1, D), lambda i: (0, 0)),
                  pl.BlockSpec((1, 1, D), lambda i: (row(i), 0, chunk)),
                  pl.BlockSpec((1, 1, D), lambda i: (row(i), 0, chunk + 1)),
                  wspec, wspec, wspec],
        out_specs=[pl.BlockSpec((tm, HALF), lambda i: (i, 0)), pl.BlockSpec((tm, LANE), lambda i: (i, 0))],
        out_shape=[jax.ShapeDtypeStruct((n_rows, HALF), jnp.uint32),
                   jax.ShapeDtypeStruct((n_rows, LANE), F32)],
        compiler_params=_cp(1), name="moe_router",
    )(x, g, mod, mod, wh, wm, wl)


def _row_copy(src_hbm, dst_ref, sem, src_row, dst_row):
    return pltpu.make_async_copy(src_hbm.at[pl.ds(src_row, 1), :], dst_ref.at[pl.ds(dst_row, 1), :], sem)


def _gather_kernel(idx_ref, src_hbm, o_ref, sem):
    base = pl.program_id(0) * TG

    def issue(g, c):
        for u in range(DMA_GROUP):
            r = g * DMA_GROUP + u
            _row_copy(src_hbm, o_ref, sem, idx_ref[base + r], r).start(priority=u % 2)
        return c

    lax.fori_loop(0, TG // DMA_GROUP, issue, 0)

    def wait(g, c):
        for u in range(DMA_GROUP):
            _row_copy(src_hbm, o_ref, sem, 0, g * DMA_GROUP + u).wait()
        return c

    lax.fori_loop(0, TG // DMA_GROUP, wait, 0)


def gather_rows(src, idx):
    n_out = idx.shape[0]
    return pl.pallas_call(
        _gather_kernel,
        grid_spec=pltpu.PrefetchScalarGridSpec(
            num_scalar_prefetch=1, grid=(n_out // TG,),
            in_specs=[pl.BlockSpec(memory_space=pl.ANY)],
            out_specs=pl.BlockSpec((TG, src.shape[1]), lambda i, idx: (i, 0)),
            scratch_shapes=[pltpu.SemaphoreType.DMA]),
        out_shape=jax.ShapeDtypeStruct((n_out, src.shape[1]), src.dtype),
        compiler_params=_cp(1), name="gather_rows",
    )(idx, src)


def _moe_kernel(te_ref, nu_ref, xs_ref, w1_ref, w3_ref, w2_ref, o_ref, xb_scr, acc_scr):
    i, j = pl.program_id(0), pl.program_id(1)
    last = pl.num_programs(1) - 1
    used = i < nu_ref[0]

    @pl.when(used)
    def _():
        @pl.when(j == 0)
        def _():
            lo, hi = _unpack_bf16_pair(xs_ref[...])
            xb_scr[:, :HALF] = lo
            xb_scr[:, HALF:] = hi
            acc_scr[...] = jnp.zeros_like(acc_scr)

        xb = xb_scr[...]
        a = _dot(xb, w1_ref[0])
        b = _dot(xb, w3_ref[0])
        h = (a * _sigmoid(a) * b).astype(BF16)
        acc_scr[...] += _dot(h, w2_ref[0])

        @pl.when(j == last)
        def _():
            o_ref[...] = acc_scr[...]

    @pl.when(jnp.logical_and(jnp.logical_not(used), j == last))
    def _():
        o_ref[...] = jnp.zeros_like(o_ref)


def moe_experts(xs, tile_expert, n_used, w1, w3, w2):
    p = xs.shape[0]
    tm, tf = TM_MOE, TF_MOE
    return pl.pallas_call(
        _moe_kernel,
        grid_spec=pltpu.PrefetchScalarGridSpec(
            num_scalar_prefetch=2, grid=(p // tm, D_FF_EXPERT // tf),
            in_specs=[pl.BlockSpec((tm, HALF), lambda i, j, te, nu: (i, 0)),
                      pl.BlockSpec((1, D, tf), lambda i, j, te, nu: (te[i], 0, j)),
                      pl.BlockSpec((1, D, tf), lambda i, j, te, nu: (te[i], 0, j)),
                      pl.BlockSpec((1, tf, D), lambda i, j, te, nu: (te[i], j, 0))],
            out_specs=pl.BlockSpec((tm, D), lambda i, j, te, nu: (i, 0)),
            scratch_shapes=[pltpu.VMEM((tm, D), BF16), pltpu.VMEM((tm, D), F32)]),
        out_shape=jax.ShapeDtypeStruct((p, D), F32),
        compiler_params=_cp(2), name="moe_experts",
    )(tile_expert, n_used, xs, w1, w3, w2)


def _combine_kernel(n_tok, pos_ref, ys_hbm, x_ref, rt_ref, gate_ref, o_ref, buf, sem):
    base = pl.program_id(0) * TG

    def issue(g, c):
        for u in range(DMA_GROUP):
            r = g * DMA_GROUP + u
            _row_copy(ys_hbm, buf.at[0], sem, pos_ref[base + r], r).start(priority=0)
            _row_copy(ys_hbm, buf.at[1], sem, pos_ref[n_tok + base + r], r).start(priority=1)
        return c

    lax.fori_loop(0, TG // DMA_GROUP, issue, 0)

    def wait(g, c):
        for u in range(DMA_GROUP):
            r = g * DMA_GROUP + u
            _row_copy(ys_hbm, buf.at[0], sem, 0, r).wait()
            _row_copy(ys_hbm, buf.at[1], sem, 0, r).wait()
        return c

    lax.fori_loop(0, TG // DMA_GROUP, wait, 0)
    rt = rt_ref[...]
    y = rt[:, 2:3] * buf[0] + rt[:, 3:4] * buf[1]
    o_ref[...] = x_ref[...] + gate_ref[0] * y


def moe_combine(ys, pos, x, rt, mod, chunk, n_rows):
    row = _mod_row(TG)
    return pl.pallas_call(
        functools.partial(_combine_kernel, n_rows),
        grid_spec=pltpu.PrefetchScalarGridSpec(
            num_scalar_prefetch=1, grid=(n_rows // TG,),
            in_specs=[pl.BlockSpec(memory_space=pl.ANY),
                      pl.BlockSpec((TG, D), lambda i, pos: (i, 0)),
                      pl.BlockSpec((TG, LANE), lambda i, pos: (i, 0)),
                      pl.BlockSpec((1, 1, D), lambda i, pos: (row(i), 0, chunk))],
            out_specs=pl.BlockSpec((TG, D), lambda i, pos: (i, 0)),
            scratch_shapes=[pltpu.VMEM((2, TG, D), F32), pltpu.SemaphoreType.DMA]),
        out_shape=jax.ShapeDtypeStruct((n_rows, D), F32),
        compiler_params=_cp(1), name="moe_combine",
    )(pos, ys, x, rt, mod)


def moe_ffn(x, g, mod, w_router, w1, w3, w2, n_rows):
    u, rt = moe_router(x, g, mod, 3, w_router, n_rows)
    tm = TM_MOE
    n_pairs = 2 * n_rows
    p_rows = n_pairs + N_EXPERTS * tm
    n_tiles = p_rows // tm
    e_flat = jnp.concatenate([rt[:, 0], rt[:, 1]]).astype(jnp.int32)
    onehot = (e_flat[:, None] == jnp.arange(N_EXPERTS, dtype=jnp.int32)[None, :]).astype(jnp.int32)
    csum = jnp.cumsum(onehot, axis=0)
    counts = csum[-1]
    rank = jnp.sum((csum - onehot) * onehot, axis=1)
    padded = ((counts + tm - 1) // tm) * tm
    g_end = jnp.cumsum(padded)
    g_start = g_end - padded
    dest = g_start[e_flat] + rank
    tok = jnp.tile(jnp.arange(n_rows, dtype=jnp.int32), 2)
    src = jnp.zeros((p_rows,), jnp.int32).at[dest].set(tok)
    tile_start = jnp.arange(n_tiles, dtype=jnp.int32) * tm
    tile_expert = jnp.minimum(jnp.sum((tile_start[:, None] >= g_end[None, :]).astype(jnp.int32), axis=1),
                              N_EXPERTS - 1).astype(jnp.int32)
    n_used = (g_end[-1] // tm).astype(jnp.int32).reshape(1)
    xs = gather_rows(u, src)
    ys = moe_experts(xs, tile_expert, n_used, w1, w3, w2)
    return moe_combine(ys, dest, x, rt, mod, 5, n_rows)


def kernel(x, c, ctx, c_ctx, w_mod, b_mod, g_norm1, g_norm2, w_in, g_diff_q, g_diff_k, diff_lambda,
           g_diff_out, w_diff_o, g_mla_cq, g_mla_ckv, w_mla_uq, w_mla_ukv, g_mla_q, g_mla_k, w_mla_o,
           w_fnet, w_out, ffn_w1, ffn_w3, ffn_w2, moe_router, moe_w1, moe_w3, moe_w2):
    xs = jnp.concatenate([x.reshape(N_LAT, D), ctx.reshape(N_CTX, D)], axis=0)
    c_all = jnp.zeros((32, D), F32).at[:BATCH].set(c).at[BATCH].set(c_ctx)
    mods = mod_tables(c_all, w_mod, b_mod)

    cos_lat, sin_lat = rope_tables(SEQ, CTX)
    cos_ctx, sin_ctx = rope_tables(0, CTX)
    cw, sw = dft_matrices(FNET_GROUP_W, FNET_GROUP_W ** -0.5)
    cn_lat, sn_lat = dft_matrices(SEQ, SEQ ** -0.5)
    cn_ctx, sn_ctx = dft_matrices(CTX, CTX ** -0.5)
    sn_lat, sn_ctx = -sn_lat, -sn_ctx

    lat_segs = [(SEQ, 0), (CTX, N_LAT // CTX)]
    ctx_segs = [(CTX, N_LAT // CTX)]

    for i in range(DEPTH):
        need_ctx = i < DEPTH - 1
        n_rows = N_TOK if need_ctx else N_LAT
        mod = mods[i].reshape(32, 1, N_MOD * D)
        wi = w_in[i]
        w_in_p = jnp.concatenate(
            [wi[:, :Z_KR], wi[:, Z_KR:Z_KR + MLA_ROPE], jnp.zeros((D, Z_GATE - Z_KR - MLA_ROPE), F32),
             wi[:, Z_KR + MLA_ROPE:]], axis=1).astype(BF16)
        w_uq = jnp.pad(w_mla_uq[i].reshape(MLA_Q_LORA, HEADS, MLA_QK),
                       ((0, 0), (0, 0), (0, 256 - MLA_QK))).reshape(MLA_Q_LORA, HEADS * 256).astype(BF16)
        w_ukv = w_mla_ukv[i].astype(BF16)
        g_q = jnp.pad(g_mla_q[i], (0, 256 - MLA_QK)).reshape(1, 256)
        g_k = jnp.pad(g_mla_k[i], (0, 256 - MLA_QK)).reshape(1, 256)
        lam_init = 0.8 - 0.6 * math.exp(-0.3 * i)
        lp = diff_lambda[i]
        lam = (jnp.exp(jnp.sum(lp[0] * lp[1])) - jnp.exp(jnp.sum(lp[2] * lp[3])) + lam_init).reshape(1)
        gdq = jnp.tile(g_diff_q[i], 2).reshape(1, LANE)
        gdk = jnp.tile(g_diff_k[i], 2).reshape(1, LANE)
        gdo = (g_diff_out[i] * (1.0 - lam_init)).reshape(1, LANE)

        z = norm_matmul(xs, g_norm1[i].reshape(1, D), mod, 0, [w_in_p], 1024, N_TOK)
        q_mla, kv = mla_up(z, g_mla_cq[i].reshape(1, -1), g_mla_ckv[i].reshape(1, -1), w_uq, w_ukv)
        lat = (fourier_mix(cn_lat, sn_lat, z, cw, sw, SEQ, 0, min(512, SEQ)),
               diff_attention(z, lam, gdq, gdk, gdo, cos_lat, sin_lat, SEQ, 0, lat_segs, TQ_DIFF),
               mla_attention(q_mla, kv, z, g_q, g_k, cos_lat, sin_lat, SEQ, 0, lat_segs, TQ_MLA))
        ctx_out = None
        if need_ctx:
            c0 = N_LAT // CTX
            ctx_out = (fourier_mix(cn_ctx, sn_ctx, z, cw, sw, CTX, c0, CTX),
                       diff_attention(z, lam, gdq, gdk, gdo, cos_ctx, sin_ctx, CTX, c0, ctx_segs, CTX),
                       mla_attention(q_mla, kv, z, g_q, g_k, cos_ctx, sin_ctx, CTX, c0, ctx_segs, CTX))
        merged = merge_branches(lat, ctx_out, z, w_fnet[i].astype(BF16), w_diff_o[i].astype(BF16),
                                w_mla_o[i].astype(BF16))
        xs = matmul_residual(merged, w_out[i].astype(BF16), xs, mod, 2, TM, 1024, n_rows)

        g2 = g_norm2[i].reshape(1, D)
        if i % 2 == 0:
            j = i // 2
            h = norm_matmul(xs, g2, mod, 3, [ffn_w1[j].astype(BF16), ffn_w3[j].astype(BF16)], 512, n_rows)
            xs = matmul_residual(h, ffn_w2[j].astype(BF16), xs, mod, 5, TM_FFN2, 512, n_rows)
        else:
            j = i // 2
            xs = moe_ffn(xs, g2, mod, moe_router[j], moe_w1[j].astype(BF16), moe_w3[j].astype(BF16),
                         moe_w2[j].astype(BF16), n_rows)
    return xs[:N_LAT].reshape(BATCH, SEQ, D)
```

```python
import functools
import math

import jax
import jax.numpy as jnp
from jax import lax
from jax.experimental import pallas as pl
from jax.experimental.pallas import tpu as pltpu

F32 = jnp.float32
BF16 = jnp.bfloat16

D = 2048
BATCH = 16
SEQ = 2048
DEPTH = 4
GRID_W = 64
CTX = 256
N_LAT = BATCH * SEQ
N_CTX = BATCH * CTX
N_TOK = N_LAT + N_CTX
HEADS = 8
FNET_W = 1024
FNET_GROUP_W = 256
MLA_Q_LORA = 512
MLA_KV_LORA = 256
MLA_NOPE = 128
MLA_ROPE = 64
MLA_QK = MLA_NOPE + MLA_ROPE
D_FF_DENSE = 5632
N_EXPERTS = 8
D_FF_EXPERT = 4096
ROPE_THETA = 10000.0
EPS = 1e-6
N_MOD = 6
LOG2E = math.log2(math.e)

Z_F, Z_DQ, Z_DK, Z_DV, Z_CQ, Z_CKV, Z_KR, Z_GATE = 0, 1024, 2048, 3072, 4096, 4608, 4864, 5120
Z_COLS = Z_GATE + 3 * D

LANE = 128
VMEM_LIMIT = 56 * 1024 * 1024

TM = 1024
TM_FFN2 = 1024
TQ_DIFF, SUB_DIFF = 2048, 256
TQ_MLA, SUB_MLA = 2048, 512
TM_MOE = 512
TF_MOE = 512
TG_GATHER = 1024
TG_COMBINE = 512
DMA_GROUP = 8


def _cp(n_axes):
    return pltpu.CompilerParams(dimension_semantics=("arbitrary",) * n_axes,
                                vmem_limit_bytes=VMEM_LIMIT)


def _dot(a, b):
    return jnp.dot(a, b, preferred_element_type=F32)


def _dot_nt(a, b):
    return lax.dot_general(a, b, (((1,), (1,)), ((), ())), preferred_element_type=F32)


def _sigmoid(x):
    return 1.0 / (1.0 + jnp.exp(-x))


def _mod_row(tm):
    return lambda i: jnp.minimum((i * tm) // SEQ, BATCH)


def _mod_kernel(c_ref, w_ref, b_ref, o_ref):
    c = c_ref[...]
    a = (c * _sigmoid(c)).astype(BF16)
    o_ref[0] = _dot(a, w_ref[0].astype(BF16)) + b_ref[0]


def mod_tables(c_all, w_mod, b_mod):
    tn = 1024
    rows = c_all.shape[0]
    return pl.pallas_call(
        _mod_kernel,
        grid=(DEPTH, N_MOD * D // tn),
        in_specs=[pl.BlockSpec((rows, D), lambda l, j: (0, 0)),
                  pl.BlockSpec((1, D, tn), lambda l, j: (l, 0, j)),
                  pl.BlockSpec((1, 1, tn), lambda l, j: (l, 0, j))],
        out_specs=pl.BlockSpec((1, rows, tn), lambda l, j: (l, 0, j)),
        out_shape=jax.ShapeDtypeStruct((DEPTH, rows, N_MOD * D), F32),
        compiler_params=_cp(2), name="mod_tables",
    )(c_all, w_mod, b_mod.reshape(DEPTH, 1, N_MOD * D))


def _modulate(x, g, shift, scale):
    ms = jnp.mean(x * x, axis=-1, keepdims=True)
    y = x * lax.rsqrt(ms + EPS) * g
    return y * (1.0 + scale) + shift


def _norm_mm_kernel(x_ref, g_ref, sh_ref, sc_ref, w_ref, o_ref, u_scr):
    @pl.when(pl.program_id(1) == 0)
    def _():
        u_scr[...] = _modulate(x_ref[...], g_ref[...], sh_ref[0], sc_ref[0]).astype(BF16)

    o_ref[...] = _dot(u_scr[...], w_ref[...]).astype(o_ref.dtype)


def _norm_swiglu_kernel(x_ref, g_ref, sh_ref, sc_ref, w1_ref, w3_ref, o_ref, u_scr):
    @pl.when(pl.program_id(1) == 0)
    def _():
        u_scr[...] = _modulate(x_ref[...], g_ref[...], sh_ref[0], sc_ref[0]).astype(BF16)

    u = u_scr[...]
    a = _dot(u, w1_ref[...])
    b = _dot(u, w3_ref[...])
    o_ref[...] = (a * _sigmoid(a) * b).astype(o_ref.dtype)


def norm_matmul(x, g, mod, chunk, ws, tn, n_rows):
    n = ws[0].shape[1]
    tm = TM
    kern = _norm_mm_kernel if len(ws) == 1 else _norm_swiglu_kernel
    row = _mod_row(tm)
    return pl.pallas_call(
        kern,
        grid=(n_rows // tm, n // tn),
        in_specs=[pl.BlockSpec((tm, D), lambda i, j: (i, 0)),
                  pl.BlockSpec((1, D), lambda i, j: (0, 0)),
                  pl.BlockSpec((1, 1, D), lambda i, j: (row(i), 0, chunk)),
                  pl.BlockSpec((1, 1, D), lambda i, j: (row(i), 0, chunk + 1))]
                 + [pl.BlockSpec((D, tn), lambda i, j: (0, j)) for _ in ws],
        out_specs=pl.BlockSpec((tm, tn), lambda i, j: (i, j)),
        out_shape=jax.ShapeDtypeStruct((n_rows, n), BF16),
        scratch_shapes=[pltpu.VMEM((tm, D), BF16)],
        compiler_params=_cp(2), name="norm_matmul%d" % len(ws),
    )(x, g, mod, mod, *ws)


def _mm_res_kernel(a_ref, w_ref, x_ref, gate_ref, o_ref):
    o_ref[...] = x_ref[...] + gate_ref[0] * _dot(a_ref[...], w_ref[...])


def matmul_residual(a, w, x, mod, chunk, tm, tn, n_rows):
    k = a.shape[1]
    row = _mod_row(tm)
    nj = D // tn
    return pl.pallas_call(
        _mm_res_kernel,
        grid=(n_rows // tm, nj),
        in_specs=[pl.BlockSpec((tm, k), lambda i, j: (i, 0)),
                  pl.BlockSpec((k, tn), lambda i, j: (0, j)),
                  pl.BlockSpec((tm, tn), lambda i, j: (i, j)),
                  pl.BlockSpec((1, 1, tn), lambda i, j: (row(i), 0, chunk * nj + j))],
        out_specs=pl.BlockSpec((tm, tn), lambda i, j: (i, j)),
        out_shape=jax.ShapeDtypeStruct((n_rows, D), F32),
        compiler_params=_cp(2), name="matmul_residual",
    )(a, w, x, mod)


def _dft_kernel(cn_ref, sn_ref, z_ref, cw_ref, sw_ref, y_ref, zc_scr, zs_scr):
    @pl.when(pl.program_id(1) == 0)
    def _():
        gw = FNET_GROUP_W
        for g in range(FNET_W // gw):
            zg = z_ref[:, g * gw:(g + 1) * gw]
            zc_scr[:, g * gw:(g + 1) * gw] = _dot(zg, cw_ref[...]).astype(BF16)
            zs_scr[:, g * gw:(g + 1) * gw] = _dot(zg, sw_ref[...]).astype(BF16)

    y_ref[...] = (_dot(cn_ref[...], zc_scr[...]) + _dot(sn_ref[...], zs_scr[...])).astype(BF16)


def fourier_mix(cn, sn, z, cw, sw, n, row_blk0, tm):
    mt = n // tm
    gw = FNET_GROUP_W
    mspec = pl.BlockSpec((tm, n), lambda b, m: (m, 0))
    wspec = pl.BlockSpec((gw, gw), lambda b, m: (0, 0))
    return pl.pallas_call(
        _dft_kernel,
        grid=(BATCH, mt),
        in_specs=[mspec, mspec, pl.BlockSpec((n, FNET_W), lambda b, m: (row_blk0 + b, 0)), wspec, wspec],
        out_specs=pl.BlockSpec((tm, FNET_W), lambda b, m: (b * mt + m, 0)),
        out_shape=jax.ShapeDtypeStruct((BATCH * n, FNET_W), BF16),
        scratch_shapes=[pltpu.VMEM((n, FNET_W), BF16), pltpu.VMEM((n, FNET_W), BF16)],
        compiler_params=_cp(2), name="fourier_mix",
    )(cn, sn, z, cw, sw)


def dft_matrices(n, scale):
    idx = jnp.arange(n, dtype=jnp.int32)
    prod = (idx[:, None] * idx[None, :]) % n
    ang = prod.astype(F32) * (2.0 * math.pi / n)
    return (jnp.cos(ang) * scale).astype(BF16), (jnp.sin(ang) * scale).astype(BF16)


def rope_tables(n_lat, n_ctx):
    n_freq = 16
    inv_freq = ROPE_THETA ** (-jnp.arange(n_freq, dtype=F32) / n_freq)
    cos_parts, sin_parts = [], []
    if n_lat:
        rows = n_lat // GRID_W
        row = jnp.repeat(jnp.arange(rows, dtype=F32), GRID_W)
        col = jnp.tile(jnp.arange(GRID_W, dtype=F32), rows)
        ang = jnp.concatenate([row[:, None] * inv_freq, col[:, None] * inv_freq], axis=-1)
        cos_parts.append(jnp.cos(ang))
        sin_parts.append(jnp.sin(ang))
    if n_ctx:
        cos_parts.append(jnp.ones((n_ctx, 32), F32))
        sin_parts.append(jnp.zeros((n_ctx, 32), F32))
    cos = jnp.concatenate(cos_parts, axis=0)
    sin = jnp.concatenate(sin_parts, axis=0)
    cos_f = jnp.tile(jnp.concatenate([cos, cos], axis=-1), (1, 2))
    sin_s = jnp.tile(jnp.concatenate([-sin, sin], axis=-1), (1, 2))
    return cos_f, sin_s


def _rope(x, cos_f, sin_s, first_half):
    swapped = jnp.where(first_half, pltpu.roll(x, 96, 1), pltpu.roll(x, 32, 1))
    return x * cos_f + swapped * sin_s


def _softmax_pv(q, k_scr, v_scr):
    s = _dot_nt(q, k_scr[...]).astype(BF16)
    m = jnp.max(s, axis=-1, keepdims=True)
    acc = _dot(jnp.exp2(s - m), v_scr[...])
    return acc[:, :LANE], acc[:, LANE:]


def _softmax_pv_t(q, k_scr, vt_scr):
    s = _dot_nt(k_scr[...], q).astype(BF16)
    m = jnp.max(s, axis=0, keepdims=True)
    acc = _dot(vt_scr[...], jnp.exp2(s - m))
    return acc[:LANE, :], acc[LANE:, :]


def _diff_attn_kernel(n_seg, seg_rows, tq, hps, *refs):
    lam_ref, q_ref = refs[0], refs[1]
    k_refs = refs[2:2 + n_seg]
    v_refs = refs[2 + n_seg:2 + 2 * n_seg]
    (gq_ref, gk_ref, go_ref, cq_ref, sq_ref, ck_ref, sk_ref, o_ref,
     k_scr, v_scr, q0_scr, q1_scr) = refs[2 + 2 * n_seg:]

    lane = lax.broadcasted_iota(jnp.int32, (1, LANE), 1)
    first_half = (lane % 64) < 32
    comp0 = lane < 64

    def head_norm(x, g):
        gi = lax.broadcasted_iota(jnp.int32, (LANE, LANE), 0) // 64
        gj = lax.broadcasted_iota(jnp.int32, (LANE, LANE), 1) // 64
        group_mean = jnp.where(gi == gj, 1.0 / 64.0, 0.0).astype(BF16)
        ms = _dot((x * x).astype(BF16), group_mean)
        return x * lax.rsqrt(ms + EPS) * g

    def prep(c0):
        r0 = 0
        for s in range(n_seg):
            n = seg_rows[s]
            k = head_norm(k_refs[s][:, c0:c0 + LANE].astype(F32), gk_ref[...])
            k = _rope(k, ck_ref[r0:r0 + n, :], sk_ref[r0:r0 + n, :], first_half)
            k_scr[r0:r0 + n, :] = k.astype(BF16)
            v_scr[r0:r0 + n, :LANE] = v_refs[s][:, c0:c0 + LANE]
            v_scr[r0:r0 + n, LANE:] = jnp.ones((n, LANE), BF16)
            r0 += n
        q = head_norm(q_ref[:, c0:c0 + LANE].astype(F32), gq_ref[...])
        q = _rope(q, cq_ref[...], sq_ref[...], first_half) * (64.0 ** -0.5 * LOG2E)
        q0_scr[...] = jnp.where(comp0, q, 0.0).astype(BF16)
        q1_scr[...] = jnp.where(comp0, 0.0, q).astype(BF16)

    sub = min(tq, SUB_DIFF)
    for hh in range(hps):
        c0 = hh * LANE
        pl.when(pl.program_id(2) == 0)(functools.partial(prep, c0))
        for j in range(tq // sub):
            row = pl.multiple_of(pl.program_id(2) * tq + j * sub, sub)
            o0, l0 = _softmax_pv(q0_scr[pl.ds(row, sub), :], k_scr, v_scr)
            o1, l1 = _softmax_pv(q1_scr[pl.ds(row, sub), :], k_scr, v_scr)
            o = o0 * (1.0 / l0) - o1 * (lam_ref[0] / l1)
            ms = jnp.mean(o * o, axis=-1, keepdims=True)
            o_ref[j * sub:(j + 1) * sub, c0:c0 + LANE] = (o * lax.rsqrt(ms + EPS) * go_ref[...]).astype(BF16)


def diff_attention(z, lam, gq, gk, go, cos_f, sin_s, n_q, q_blk0, segs, tq, hps=1):
    n_seg = len(segs)
    seg_rows = tuple(r for r, _ in segs)
    n_k = sum(seg_rows)
    tq = min(tq, n_q)
    nqb = n_q // tq
    w = hps * LANE
    hq, hk, hv = Z_DQ // w, Z_DK // w, Z_DV // w

    def kspec(rows, off, col0):
        return pl.BlockSpec((rows, w), lambda b, h, qb: (off + b, col0 + h))

    vec = pl.BlockSpec((1, LANE), lambda b, h, qb: (0, 0))
    qtab = pl.BlockSpec((n_q, LANE), lambda b, h, qb: (0, 0))
    ktab = pl.BlockSpec((n_k, LANE), lambda b, h, qb: (0, 0))
    in_specs = ([pl.BlockSpec(memory_space=pltpu.SMEM), kspec(n_q, q_blk0, hq)]
                + [kspec(r, off, hk) for r, off in segs]
                + [kspec(r, off, hv) for r, off in segs]
                + [vec, vec, vec, qtab, qtab, ktab, ktab])
    return pl.pallas_call(
        functools.partial(_diff_attn_kernel, n_seg, seg_rows, tq, hps),
        grid=(BATCH, HEADS // hps, nqb),
        in_specs=in_specs,
        out_specs=pl.BlockSpec((tq, w), lambda b, h, qb: (b * nqb + qb, h)),
        out_shape=jax.ShapeDtypeStruct((BATCH * n_q, HEADS * LANE), BF16),
        scratch_shapes=[pltpu.VMEM((n_k, LANE), BF16), pltpu.VMEM((n_k, 2 * LANE), BF16),
                        pltpu.VMEM((n_q, LANE), BF16), pltpu.VMEM((n_q, LANE), BF16)],
        compiler_params=_cp(3), name="diff_attention",
    )(lam, z, *([z] * (2 * n_seg)), gq, gk, go, cos_f, sin_s, cos_f, sin_s)


def _mla_up_kernel(cq_ref, ckv_ref, gq_ref, gkv_ref, wq_ref, wkv_ref, q_ref, kv_ref):
    def norm(x, g):
        ms = jnp.mean(x * x, axis=-1, keepdims=True)
        return (x * lax.rsqrt(ms + EPS) * g).astype(BF16)

    q_ref[...] = _dot(norm(cq_ref[...].astype(F32), gq_ref[...]), wq_ref[...]).astype(BF16)
    kv_ref[...] = _dot(norm(ckv_ref[...].astype(F32), gkv_ref[...]), wkv_ref[...]).astype(BF16)


def mla_up(z, g_cq, g_ckv, w_uq, w_ukv):
    tm = 1024
    n = HEADS * 256
    return pl.pallas_call(
        _mla_up_kernel,
        grid=(N_TOK // tm,),
        in_specs=[pl.BlockSpec((tm, MLA_Q_LORA), lambda i: (i, Z_CQ // MLA_Q_LORA)),
                  pl.BlockSpec((tm, MLA_KV_LORA), lambda i: (i, Z_CKV // MLA_KV_LORA)),
                  pl.BlockSpec((1, MLA_Q_LORA), lambda i: (0, 0)),
                  pl.BlockSpec((1, MLA_KV_LORA), lambda i: (0, 0)),
                  pl.BlockSpec((MLA_Q_LORA, n), lambda i: (0, 0)),
                  pl.BlockSpec((MLA_KV_LORA, n), lambda i: (0, 0))],
        out_specs=[pl.BlockSpec((tm, n), lambda i: (i, 0))] * 2,
        out_shape=[jax.ShapeDtypeStruct((N_TOK, n), BF16)] * 2,
        compiler_params=_cp(1), name="mla_up",
    )(z, z, g_cq, g_ckv, w_uq, w_ukv)


def _mla_attn_kernel(n_seg, seg_rows, tq, hps, *refs):
    q_ref = refs[0]
    kv_refs = refs[1:1 + n_seg]
    kr_refs = refs[1 + n_seg:1 + 2 * n_seg]
    gq_ref, gk_ref, cq_ref, sq_ref, ck_ref, sk_ref, o_ref, k_scr, vt_scr, q_scr = refs[1 + 2 * n_seg:]

    lane = lax.broadcasted_iota(jnp.int32, (1, LANE), 1)
    first_half = (lane % 64) < 32

    def qk_norm(nope, rope, g_ref, scale):
        ones = jnp.ones((LANE, LANE), BF16)
        ss = _dot((nope * nope).astype(BF16), ones) + _dot((rope * rope).astype(BF16), ones)
        inv = lax.rsqrt(ss * (1.0 / MLA_QK) + EPS) * scale
        return ((nope * inv * g_ref[:, :LANE]).astype(BF16), (rope * inv * g_ref[:, LANE:]).astype(BF16))

    def prep(c0):
        r0 = 0
        for s in range(n_seg):
            n = seg_rows[s]
            rope = _rope(kr_refs[s][...].astype(F32), ck_ref[r0:r0 + n, :], sk_ref[r0:r0 + n, :], first_half)
            kn, kr = qk_norm(kv_refs[s][:, c0:c0 + LANE].astype(F32), rope, gk_ref, 1.0)
            k_scr[r0:r0 + n, :LANE] = kn
            k_scr[r0:r0 + n, LANE:] = kr
            vt_scr[:LANE, r0:r0 + n] = kv_refs[s][:, c0 + LANE:c0 + 2 * LANE].astype(F32).T.astype(BF16)
            vt_scr[LANE:, r0:r0 + n] = jnp.ones((LANE, n), BF16)
            r0 += n
        rope = _rope(q_ref[:, c0 + LANE:c0 + 2 * LANE].astype(F32), cq_ref[...], sq_ref[...], first_half)
        qn, qr = qk_norm(q_ref[:, c0:c0 + LANE].astype(F32), rope, gq_ref, MLA_QK ** -0.5 * LOG2E)
        q_scr[:, :LANE] = qn
        q_scr[:, LANE:] = qr

    sub = min(tq, SUB_MLA)
    for hh in range(hps):
        pl.when(pl.program_id(2) == 0)(functools.partial(prep, hh * 2 * LANE))
        for j in range(tq // sub):
            row = pl.multiple_of(pl.program_id(2) * tq + j * sub, sub)
            o, l = _softmax_pv_t(q_scr[pl.ds(row, sub), :], k_scr, vt_scr)
            o_ref[j * sub:(j + 1) * sub, hh * LANE:(hh + 1) * LANE] = (o * (1.0 / l)).T.astype(BF16)


def mla_attention(q_mla, kv, z, gq, gk, cos_f, sin_s, n_q, q_blk0, segs, tq, hps=1):
    n_seg = len(segs)
    seg_rows = tuple(r for r, _ in segs)
    n_k = sum(seg_rows)
    tq = min(tq, n_q)
    nqb = n_q // tq
    ckr = Z_KR // LANE
    w = hps * 2 * LANE
    vec = pl.BlockSpec((1, 2 * LANE), lambda b, h, qb: (0, 0))
    qtab = pl.BlockSpec((n_q, LANE), lambda b, h, qb: (0, 0))
    ktab = pl.BlockSpec((n_k, LANE), lambda b, h, qb: (0, 0))
    in_specs = ([pl.BlockSpec((n_q, w), lambda b, h, qb: (q_blk0 + b, h))]
                + [pl.BlockSpec((r, w), (lambda off: lambda b, h, qb: (off + b, h))(off)) for r, off in segs]
                + [pl.BlockSpec((r, LANE), (lambda off: lambda b, h, qb: (off + b, ckr))(off)) for r, off in segs]
                + [vec, vec, qtab, qtab, ktab, ktab])
    return pl.pallas_call(
        functools.partial(_mla_attn_kernel, n_seg, seg_rows, tq, hps),
        grid=(BATCH, HEADS // hps, nqb),
        in_specs=in_specs,
        out_specs=pl.BlockSpec((tq, hps * LANE), lambda b, h, qb: (b * nqb + qb, h)),
        out_shape=jax.ShapeDtypeStruct((BATCH * n_q, HEADS * LANE), BF16),
        scratch_shapes=[pltpu.VMEM((n_k, 2 * LANE), BF16), pltpu.VMEM((2 * LANE, n_k), BF16),
                        pltpu.VMEM((n_q, 2 * LANE), BF16)],
        compiler_params=_cp(3), name="mla_attention",
    )(q_mla, *([kv] * n_seg), *([z] * n_seg), gq, gk, cos_f, sin_s, cos_f, sin_s)


def _merge_kernel(n_lat_tiles, n_src, *refs):
    srcs = [refs[3 * s:3 * s + 3] for s in range(n_src)]
    g0_ref, g1_ref, g2_ref, wf_ref, wd_ref, wm_ref, o_ref = refs[3 * n_src:]

    def body(yf_ref, od_ref, om_ref):
        f = _dot(yf_ref[...], wf_ref[...])
        d = _dot(od_ref[...], wd_ref[...])
        m = _dot(om_ref[...], wm_ref[...])
        o = (_sigmoid(g0_ref[...].astype(F32)) * f + _sigmoid(g1_ref[...].astype(F32)) * d
             + _sigmoid(g2_ref[...].astype(F32)) * m)
        o_ref[...] = o.astype(BF16)

    if n_src == 1:
        body(*srcs[0])
    else:
        i = pl.program_id(0)
        pl.when(i < n_lat_tiles)(lambda: body(*srcs[0]))
        pl.when(i >= n_lat_tiles)(lambda: body(*srcs[1]))


def merge_branches(lat, ctx, z, w_f, w_d, w_m):
    tm, tn = TM, 512
    k = FNET_W
    nl = N_LAT // tm
    n_rows = N_LAT + (N_CTX if ctx is not None else 0)
    w_spec = pl.BlockSpec((k, tn), lambda i, j: (0, j))
    g0 = Z_GATE // tn

    def gspec(r):
        return pl.BlockSpec((tm, tn), lambda i, j: (i, g0 + r * (D // tn) + j))

    lat_spec = pl.BlockSpec((tm, k), lambda i, j: (jnp.minimum(i, nl - 1), 0))
    ctx_spec = pl.BlockSpec((tm, k), lambda i, j: (jnp.maximum(i - nl, 0), 0))
    srcs = list(lat) + (list(ctx) if ctx is not None else [])
    specs = [lat_spec] * 3 + ([ctx_spec] * 3 if ctx is not None else [])
    return pl.pallas_call(
        functools.partial(_merge_kernel, nl, 2 if ctx is not None else 1),
        grid=(n_rows // tm, D // tn),
        in_specs=specs + [gspec(0), gspec(1), gspec(2), w_spec, w_spec, w_spec],
        out_specs=pl.BlockSpec((tm, tn), lambda i, j: (i, j)),
        out_shape=jax.ShapeDtypeStruct((n_rows, D), BF16),
        compiler_params=_cp(2), name="merge_branches",
    )(*srcs, z, z, z, w_f, w_d, w_m)


def _split3(x):
    hi = x.astype(BF16)
    r = x - hi.astype(F32)
    mid = r.astype(BF16)
    lo = (r - mid.astype(F32)).astype(BF16)
    return hi, mid, lo


HALF = D // 2
HI16 = 0xFFFF0000


def _pack_bf16_pair(x):
    lo = lax.bitcast_convert_type(x[:, :HALF].astype(BF16).astype(F32), jnp.uint32) >> 16
    hi = lax.bitcast_convert_type(x[:, HALF:].astype(BF16).astype(F32), jnp.uint32) & jnp.uint32(HI16)
    return lo | hi


def _unpack_bf16_pair(p):
    lo = lax.bitcast_convert_type(p << 16, F32).astype(BF16)
    hi = lax.bitcast_convert_type(p & jnp.uint32(HI16), F32).astype(BF16)
    return lo, hi


def _router_kernel(x_ref, g_ref, sh_ref, sc_ref, wh_ref, wm_ref, wl_ref, u_ref, rt_ref):
    u = _modulate(x_ref[...], g_ref[...], sh_ref[0], sc_ref[0])
    u_ref[...] = _pack_bf16_pair(u)
    uh, um, ul = _split3(u)
    wh, wm, wl = wh_ref[...], wm_ref[...], wl_ref[...]
    logits = (_dot(ul, wh) + _dot(uh, wl) + _dot(um, wm)) + (_dot(um, wh) + _dot(uh, wm)) + _dot(uh, wh)
    lane = lax.broadcasted_iota(jnp.int32, logits.shape, 1).astype(F32)
    neg = -jnp.inf
    lg = jnp.where(lane < N_EXPERTS, logits, neg)
    m1 = jnp.max(lg, axis=-1, keepdims=True)
    i1 = jnp.min(jnp.where(lg == m1, lane, float(LANE)), axis=-1, keepdims=True)
    lg2 = jnp.where(lane == i1, neg, lg)
    m2 = jnp.max(lg2, axis=-1, keepdims=True)
    i2 = jnp.min(jnp.where(lg2 == m2, lane, float(LANE)), axis=-1, keepdims=True)
    p1 = 1.0 / (1.0 + jnp.exp(m2 - m1))
    p2 = 1.0 - p1
    rt = jnp.where(lane == 0.0, i1, jnp.where(lane == 1.0, i2, jnp.where(lane == 2.0, p1,
                                                                      jnp.where(lane == 3.0, p2, 0.0))))
    rt_ref[...] = rt


def moe_router(x, g, mod, chunk, w_router, n_rows):
    tm = 512
    row = _mod_row(tm)
    wpad = jnp.pad(w_router, ((0, 0), (0, LANE - N_EXPERTS)))
    wh, wm, wl = _split3(wpad)
    wspec = pl.BlockSpec((D, LANE), lambda i: (0, 0))
    return pl.pallas_call(
        _router_kernel,
        grid=(n_rows // tm,),
        in_specs=[pl.BlockSpec((tm, D), lambda i: (i, 0)),
                  pl.BlockSpec((1, D), lambda i: (0, 0)),
                  pl.BlockSpec((1, 1, D), lambda i: (row(i), 0, chunk)),
                  pl.BlockSpec((1, 1, D), lambda i: (row(i), 0, chunk + 1)),
                  wspec, wspec, wspec],
        out_specs=[pl.BlockSpec((tm, HALF), lambda i: (i, 0)), pl.BlockSpec((tm, LANE), lambda i: (i, 0))],
        out_shape=[jax.ShapeDtypeStruct((n_rows, HALF), jnp.uint32),
                   jax.ShapeDtypeStruct((n_rows, LANE), F32)],
        compiler_params=_cp(1), name="moe_router",
    )(x, g, mod, mod, wh, wm, wl)


def _row_copy(src_hbm, dst_ref, sem, src_row, dst_row):
    return pltpu.make_async_copy(src_hbm.at[pl.ds(src_row, 1), :], dst_ref.at[pl.ds(dst_row, 1), :], sem)


def _gather_kernel(idx_ref, src_hbm, o_ref, sem):
    tg = o_ref.shape[0]
    base = pl.program_id(0) * tg

    def issue(g, c):
        for u in range(DMA_GROUP):
            r = g * DMA_GROUP + u
            _row_copy(src_hbm, o_ref, sem, idx_ref[base + r], r).start(priority=u % 2)
        return c

    lax.fori_loop(0, tg // DMA_GROUP, issue, 0)

    def wait(g, c):
        for u in range(DMA_GROUP):
            _row_copy(src_hbm, o_ref, sem, 0, g * DMA_GROUP + u).wait()
        return c

    lax.fori_loop(0, tg // DMA_GROUP, wait, 0)


def gather_rows(src, idx):
    n_out = idx.shape[0]
    return pl.pallas_call(
        _gather_kernel,
        grid_spec=pltpu.PrefetchScalarGridSpec(
            num_scalar_prefetch=1, grid=(n_out // TG_GATHER,),
            in_specs=[pl.BlockSpec(memory_space=pl.ANY)],
            out_specs=pl.BlockSpec((TG_GATHER, src.shape[1]), lambda i, idx: (i, 0)),
            scratch_shapes=[pltpu.SemaphoreType.DMA]),
        out_shape=jax.ShapeDtypeStruct((n_out, src.shape[1]), src.dtype),
        compiler_params=_cp(1), name="gather_rows",
    )(idx, src)


def _moe_kernel(te_ref, nu_ref, xs_ref, w1_ref, w3_ref, w2_ref, o_ref, xb_scr, acc_scr):
    i, j = pl.program_id(0), pl.program_id(1)
    last = pl.num_programs(1) - 1
    used = i < nu_ref[0]

    @pl.when(used)
    def _():
        @pl.when(j == 0)
        def _():
            lo, hi = _unpack_bf16_pair(xs_ref[...])
            xb_scr[:, :HALF] = lo
            xb_scr[:, HALF:] = hi
            acc_scr[...] = jnp.zeros_like(acc_scr)

        xb = xb_scr[...]
        a = _dot(xb, w1_ref[0])
        b = _dot(xb, w3_ref[0])
        h = (a * _sigmoid(a) * b).astype(BF16)
        acc_scr[...] += _dot(h, w2_ref[0])

        @pl.when(j == last)
        def _():
            o_ref[...] = acc_scr[...]

    @pl.when(jnp.logical_and(jnp.logical_not(used), j == last))
    def _():
        o_ref[...] = jnp.zeros_like(o_ref)


def moe_experts(xs, tile_expert, n_used, w1, w3, w2):
    p = xs.shape[0]
    tm, tf = TM_MOE, TF_MOE
    return pl.pallas_call(
        _moe_kernel,
        grid_spec=pltpu.PrefetchScalarGridSpec(
            num_scalar_prefetch=2, grid=(p // tm, D_FF_EXPERT // tf),
            in_specs=[pl.BlockSpec((tm, HALF), lambda i, j, te, nu: (i, 0)),
                      pl.BlockSpec((1, D, tf), lambda i, j, te, nu: (te[i], 0, j)),
                      pl.BlockSpec((1, D, tf), lambda i, j, te, nu: (te[i], 0, j)),
                      pl.BlockSpec((1, tf, D), lambda i, j, te, nu: (te[i], j, 0))],
            out_specs=pl.BlockSpec((tm, D), lambda i, j, te, nu: (i, 0)),
            scratch_shapes=[pltpu.VMEM((tm, D), BF16), pltpu.VMEM((tm, D), F32)]),
        out_shape=jax.ShapeDtypeStruct((p, D), F32),
        compiler_params=_cp(2), name="moe_experts",
    )(tile_expert, n_used, xs, w1, w3, w2)


def _combine_kernel(n_tok, pos_ref, ys_hbm, x_ref, rt_ref, gate_ref, o_ref, buf, sem):
    tg = o_ref.shape[0]
    base = pl.program_id(0) * tg

    def issue(g, c):
        for u in range(DMA_GROUP):
            r = g * DMA_GROUP + u
            _row_copy(ys_hbm, buf.at[0], sem, pos_ref[base + r], r).start(priority=0)
            _row_copy(ys_hbm, buf.at[1], sem, pos_ref[n_tok + base + r], r).start(priority=1)
        return c

    lax.fori_loop(0, tg // DMA_GROUP, issue, 0)

    def wait(g, c):
        for u in range(DMA_GROUP):
            r = g * DMA_GROUP + u
            _row_copy(ys_hbm, buf.at[0], sem, 0, r).wait()
            _row_copy(ys_hbm, buf.at[1], sem, 0, r).wait()
        return c

    lax.fori_loop(0, tg // DMA_GROUP, wait, 0)
    rt = rt_ref[...]
    y = rt[:, 2:3] * buf[0] + rt[:, 3:4] * buf[1]
    o_ref[...] = x_ref[...] + gate_ref[0] * y


def moe_combine(ys, pos, x, rt, mod, chunk, n_rows):
    tg = TG_COMBINE
    row = _mod_row(tg)
    return pl.pallas_call(
        functools.partial(_combine_kernel, n_rows),
        grid_spec=pltpu.PrefetchScalarGridSpec(
            num_scalar_prefetch=1, grid=(n_rows // tg,),
            in_specs=[pl.BlockSpec(memory_space=pl.ANY),
                      pl.BlockSpec((tg, D), lambda i, pos: (i, 0)),
                      pl.BlockSpec((tg, LANE), lambda i, pos: (i, 0)),
                      pl.BlockSpec((1, 1, D), lambda i, pos: (row(i), 0, chunk))],
            out_specs=pl.BlockSpec((tg, D), lambda i, pos: (i, 0)),
            scratch_shapes=[pltpu.VMEM((2, tg, D), F32), pltpu.SemaphoreType.DMA]),
        out_shape=jax.ShapeDtypeStruct((n_rows, D), F32),
        compiler_params=_cp(1), name="moe_combine",
    )(pos, ys, x, rt, mod)


def moe_ffn(x, g, mod, w_router, w1, w3, w2, n_rows):
    u, rt = moe_router(x, g, mod, 3, w_router, n_rows)
    tm = TM_MOE
    n_pairs = 2 * n_rows
    p_rows = n_pairs + N_EXPERTS * tm
    n_tiles = p_rows // tm
    e_flat = jnp.concatenate([rt[:, 0], rt[:, 1]]).astype(jnp.int32)
    onehot = (e_flat[:, None] == jnp.arange(N_EXPERTS, dtype=jnp.int32)[None, :]).astype(jnp.int32)
    csum = jnp.cumsum(onehot, axis=0)
    counts = csum[-1]
    rank = jnp.sum((csum - onehot) * onehot, axis=1)
    padded = ((counts + tm - 1) // tm) * tm
    g_end = jnp.cumsum(padded)
    g_start = g_end - padded
    dest = g_start[e_flat] + rank
    tok = jnp.tile(jnp.arange(n_rows, dtype=jnp.int32), 2)
    src = jnp.zeros((p_rows,), jnp.int32).at[dest].set(tok)
    tile_start = jnp.arange(n_tiles, dtype=jnp.int32) * tm
    tile_expert = jnp.minimum(jnp.sum((tile_start[:, None] >= g_end[None, :]).astype(jnp.int32), axis=1),
                              N_EXPERTS - 1).astype(jnp.int32)
    n_used = (g_end[-1] // tm).astype(jnp.int32).reshape(1)
    xs = gather_rows(u, src)
    ys = moe_experts(xs, tile_expert, n_used, w1, w3, w2)
    return moe_combine(ys, dest, x, rt, mod, 5, n_rows)


def kernel(x, c, ctx, c_ctx, w_mod, b_mod, g_norm1, g_norm2, w_in, g_diff_q, g_diff_k, diff_lambda,
           g_diff_out, w_diff_o, g_mla_cq, g_mla_ckv, w_mla_uq, w_mla_ukv, g_mla_q, g_mla_k, w_mla_o,
           w_fnet, w_out, ffn_w1, ffn_w3, ffn_w2, moe_router, moe_w1, moe_w3, moe_w2):
    xs = jnp.concatenate([x.reshape(N_LAT, D), ctx.reshape(N_CTX, D)], axis=0)
    c_all = jnp.zeros((32, D), F32).at[:BATCH].set(c).at[BATCH].set(c_ctx)
    mods = mod_tables(c_all, w_mod, b_mod)

    cos_lat, sin_lat = rope_tables(SEQ, CTX)
    cos_ctx, sin_ctx = rope_tables(0, CTX)
    cw, sw = dft_matrices(FNET_GROUP_W, FNET_GROUP_W ** -0.5)
    cn_lat, sn_lat = dft_matrices(SEQ, SEQ ** -0.5)
    cn_ctx, sn_ctx = dft_matrices(CTX, CTX ** -0.5)
    sn_lat, sn_ctx = -sn_lat, -sn_ctx

    lat_segs = [(SEQ, 0), (CTX, N_LAT // CTX)]
    ctx_segs = [(CTX, N_LAT // CTX)]

    for i in range(DEPTH):
        need_ctx = i < DEPTH - 1
        n_rows = N_TOK if need_ctx else N_LAT
        mod = mods[i].reshape(32, 1, N_MOD * D)
        wi = w_in[i]
        w_in_p = jnp.concatenate(
            [wi[:, :Z_KR], wi[:, Z_KR:Z_KR + MLA_ROPE], jnp.zeros((D, Z_GATE - Z_KR - MLA_ROPE), F32),
             wi[:, Z_KR + MLA_ROPE:]], axis=1).astype(BF16)
        w_uq = jnp.pad(w_mla_uq[i].reshape(MLA_Q_LORA, HEADS, MLA_QK),
                       ((0, 0), (0, 0), (0, 256 - MLA_QK))).reshape(MLA_Q_LORA, HEADS * 256).astype(BF16)
        w_ukv = w_mla_ukv[i].astype(BF16)
        g_q = jnp.pad(g_mla_q[i], (0, 256 - MLA_QK)).reshape(1, 256)
        g_k = jnp.pad(g_mla_k[i], (0, 256 - MLA_QK)).reshape(1, 256)
        lam_init = 0.8 - 0.6 * math.exp(-0.3 * i)
        lp = diff_lambda[i]
        lam = (jnp.exp(jnp.sum(lp[0] * lp[1])) - jnp.exp(jnp.sum(lp[2] * lp[3])) + lam_init).reshape(1)
        gdq = jnp.tile(g_diff_q[i], 2).reshape(1, LANE)
        gdk = jnp.tile(g_diff_k[i], 2).reshape(1, LANE)
        gdo = (g_diff_out[i] * (1.0 - lam_init)).reshape(1, LANE)

        z = norm_matmul(xs, g_norm1[i].reshape(1, D), mod, 0, [w_in_p], 1024, N_TOK)
        q_mla, kv = mla_up(z, g_mla_cq[i].reshape(1, -1), g_mla_ckv[i].reshape(1, -1), w_uq, w_ukv)
        lat = (fourier_mix(cn_lat, sn_lat, z, cw, sw, SEQ, 0, min(512, SEQ)),
               diff_attention(z, lam, gdq, gdk, gdo, cos_lat, sin_lat, SEQ, 0, lat_segs, TQ_DIFF),
               mla_attention(q_mla, kv, z, g_q, g_k, cos_lat, sin_lat, SEQ, 0, lat_segs, TQ_MLA))
        ctx_out = None
        if need_ctx:
            c0 = N_LAT // CTX
            ctx_out = (fourier_mix(cn_ctx, sn_ctx, z, cw, sw, CTX, c0, CTX),
                       diff_attention(z, lam, gdq, gdk, gdo, cos_ctx, sin_ctx, CTX, c0, ctx_segs, CTX, HEADS),
                       mla_attention(q_mla, kv, z, g_q, g_k, cos_ctx, sin_ctx, CTX, c0, ctx_segs, CTX, HEADS))
        merged = merge_branches(lat, ctx_out, z, w_fnet[i].astype(BF16), w_diff_o[i].astype(BF16),
                                w_mla_o[i].astype(BF16))
        xs = matmul_residual(merged, w_out[i].astype(BF16), xs, mod, 2, TM, 1024, n_rows)

        g2 = g_norm2[i].reshape(1, D)
        if i % 2 == 0:
            j = i // 2
            h = norm_matmul(xs, g2, mod, 3, [ffn_w1[j].astype(BF16), ffn_w3[j].astype(BF16)], 512, n_rows)
            xs = matmul_residual(h, ffn_w2[j].astype(BF16), xs, mod, 5, TM_FFN2, 512, n_rows)
        else:
            j = i // 2
            xs = moe_ffn(xs, g2, mod, moe_router[j], moe_w1[j].astype(BF16), moe_w3[j].astype(BF16),
                         moe_w2[j].astype(BF16), n_rows)
    return xs[:N_LAT].reshape(BATCH, SEQ, D)
```

```python
import functools
import math

import jax
import jax.numpy as jnp
from jax import lax
from jax.experimental import pallas as pl
from jax.experimental.pallas import tpu as pltpu

F32 = jnp.float32
BF16 = jnp.bfloat16

D = 2048
BATCH = 16
SEQ = 2048
DEPTH = 4
GRID_W = 64
CTX = 256
N_LAT = BATCH * SEQ
N_CTX = BATCH * CTX
N_TOK = N_LAT + N_CTX
HEADS = 8
FNET_W = 1024
FNET_GROUP_W = 256
MLA_Q_LORA = 512
MLA_KV_LORA = 256
MLA_NOPE = 128
MLA_ROPE = 64
MLA_QK = MLA_NOPE + MLA_ROPE
D_FF_DENSE = 5632
N_EXPERTS = 8
D_FF_EXPERT = 4096
ROPE_THETA = 10000.0
EPS = 1e-6
N_MOD = 6
LOG2E = math.log2(math.e)

Z_F, Z_DQ, Z_DK, Z_DV, Z_CQ, Z_CKV, Z_KR, Z_GATE = 0, 1024, 2048, 3072, 4096, 4608, 4864, 5120
Z_COLS = Z_GATE + 3 * D

LANE = 128
VMEM_LIMIT = 56 * 1024 * 1024

TM = 1024
TM_FFN2 = 1024
TQ_DIFF, SUB_DIFF = 2048, 256
TQ_MLA, SUB_MLA = 2048, 256
TM_MOE = 512
TF_MOE = 512
TG_GATHER = 2048
TG_COMBINE = 512
DMA_GROUP = 8


def _cp(n_axes):
    return pltpu.CompilerParams(dimension_semantics=("arbitrary",) * n_axes,
                                vmem_limit_bytes=VMEM_LIMIT)


def _dot(a, b):
    return jnp.dot(a, b, preferred_element_type=F32)


def _dot_nt(a, b):
    return lax.dot_general(a, b, (((1,), (1,)), ((), ())), preferred_element_type=F32)


def _sigmoid(x):
    return 1.0 / (1.0 + jnp.exp(-x))


def _mod_row(tm):
    return lambda i: jnp.minimum((i * tm) // SEQ, BATCH)


def _mod_kernel(c_ref, w_ref, b_ref, o_ref):
    c = c_ref[...]
    a = (c * _sigmoid(c)).astype(BF16)
    o_ref[0] = _dot(a, w_ref[0].astype(BF16)) + b_ref[0]


def mod_tables(c_all, w_mod, b_mod):
    tn = 1024
    rows = c_all.shape[0]
    return pl.pallas_call(
        _mod_kernel,
        grid=(DEPTH, N_MOD * D // tn),
        in_specs=[pl.BlockSpec((rows, D), lambda l, j: (0, 0)),
                  pl.BlockSpec((1, D, tn), lambda l, j: (l, 0, j)),
                  pl.BlockSpec((1, 1, tn), lambda l, j: (l, 0, j))],
        out_specs=pl.BlockSpec((1, rows, tn), lambda l, j: (l, 0, j)),
        out_shape=jax.ShapeDtypeStruct((DEPTH, rows, N_MOD * D), F32),
        compiler_params=_cp(2), name="mod_tables",
    )(c_all, w_mod, b_mod.reshape(DEPTH, 1, N_MOD * D))


def _modulate(x, g, shift, scale):
    ms = jnp.mean(x * x, axis=-1, keepdims=True)
    y = x * lax.rsqrt(ms + EPS) * g
    return y * (1.0 + scale) + shift


def _norm_mm_kernel(x_ref, g_ref, sh_ref, sc_ref, w_ref, o_ref, u_scr):
    @pl.when(pl.program_id(1) == 0)
    def _():
        u_scr[...] = _modulate(x_ref[...], g_ref[...], sh_ref[0], sc_ref[0]).astype(BF16)

    o_ref[...] = _dot(u_scr[...], w_ref[...]).astype(o_ref.dtype)


def _norm_swiglu_kernel(x_ref, g_ref, sh_ref, sc_ref, w1_ref, w3_ref, o_ref, u_scr):
    @pl.when(pl.program_id(1) == 0)
    def _():
        u_scr[...] = _modulate(x_ref[...], g_ref[...], sh_ref[0], sc_ref[0]).astype(BF16)

    u = u_scr[...]
    a = _dot(u, w1_ref[...])
    b = _dot(u, w3_ref[...])
    o_ref[...] = (a * _sigmoid(a) * b).astype(o_ref.dtype)


def norm_matmul(x, g, mod, chunk, ws, tn, n_rows):
    n = ws[0].shape[1]
    tm = TM
    kern = _norm_mm_kernel if len(ws) == 1 else _norm_swiglu_kernel
    row = _mod_row(tm)
    return pl.pallas_call(
        kern,
        grid=(n_rows // tm, n // tn),
        in_specs=[pl.BlockSpec((tm, D), lambda i, j: (i, 0)),
                  pl.BlockSpec((1, D), lambda i, j: (0, 0)),
                  pl.BlockSpec((1, 1, D), lambda i, j: (row(i), 0, chunk)),
                  pl.BlockSpec((1, 1, D), lambda i, j: (row(i), 0, chunk + 1))]
                 + [pl.BlockSpec((D, tn), lambda i, j: (0, j)) for _ in ws],
        out_specs=pl.BlockSpec((tm, tn), lambda i, j: (i, j)),
        out_shape=jax.ShapeDtypeStruct((n_rows, n), BF16),
        scratch_shapes=[pltpu.VMEM((tm, D), BF16)],
        compiler_params=_cp(2), name="norm_matmul%d" % len(ws),
    )(x, g, mod, mod, *ws)


def _mm_res_kernel(a_ref, w_ref, x_ref, gate_ref, o_ref):
    o_ref[...] = x_ref[...] + gate_ref[0] * _dot(a_ref[...], w_ref[...])


def matmul_residual(a, w, x, mod, chunk, tm, tn, n_rows):
    k = a.shape[1]
    row = _mod_row(tm)
    nj = D // tn
    return pl.pallas_call(
        _mm_res_kernel,
        grid=(n_rows // tm, nj),
        in_specs=[pl.BlockSpec((tm, k), lambda i, j: (i, 0)),
                  pl.BlockSpec((k, tn), lambda i, j: (0, j)),
                  pl.BlockSpec((tm, tn), lambda i, j: (i, j)),
                  pl.BlockSpec((1, 1, tn), lambda i, j: (row(i), 0, chunk * nj + j))],
        out_specs=pl.BlockSpec((tm, tn), lambda i, j: (i, j)),
        out_shape=jax.ShapeDtypeStruct((n_rows, D), F32),
        compiler_params=_cp(2), name="matmul_residual",
    )(a, w, x, mod)


def _dft_kernel(cn_ref, sn_ref, z_ref, cw_ref, sw_ref, y_ref, zc_scr, zs_scr):
    @pl.when(pl.program_id(1) == 0)
    def _():
        gw = FNET_GROUP_W
        for g in range(FNET_W // gw):
            zg = z_ref[:, g * gw:(g + 1) * gw]
            zc_scr[:, g * gw:(g + 1) * gw] = _dot(zg, cw_ref[...]).astype(BF16)
            zs_scr[:, g * gw:(g + 1) * gw] = _dot(zg, sw_ref[...]).astype(BF16)

    y_ref[...] = (_dot(cn_ref[...], zc_scr[...]) + _dot(sn_ref[...], zs_scr[...])).astype(BF16)


def fourier_mix(cn, sn, z, cw, sw, n, row_blk0, tm):
    mt = n // tm
    gw = FNET_GROUP_W
    mspec = pl.BlockSpec((tm, n), lambda b, m: (m, 0))
    wspec = pl.BlockSpec((gw, gw), lambda b, m: (0, 0))
    return pl.pallas_call(
        _dft_kernel,
        grid=(BATCH, mt),
        in_specs=[mspec, mspec, pl.BlockSpec((n, FNET_W), lambda b, m: (row_blk0 + b, 0)), wspec, wspec],
        out_specs=pl.BlockSpec((tm, FNET_W), lambda b, m: (b * mt + m, 0)),
        out_shape=jax.ShapeDtypeStruct((BATCH * n, FNET_W), BF16),
        scratch_shapes=[pltpu.VMEM((n, FNET_W), BF16), pltpu.VMEM((n, FNET_W), BF16)],
        compiler_params=_cp(2), name="fourier_mix",
    )(cn, sn, z, cw, sw)


def dft_matrices(n, scale):
    idx = jnp.arange(n, dtype=jnp.int32)
    prod = (idx[:, None] * idx[None, :]) % n
    ang = prod.astype(F32) * (2.0 * math.pi / n)
    return (jnp.cos(ang) * scale).astype(BF16), (jnp.sin(ang) * scale).astype(BF16)


def rope_tables(n_lat, n_ctx):
    n_freq = 16
    inv_freq = ROPE_THETA ** (-jnp.arange(n_freq, dtype=F32) / n_freq)
    cos_parts, sin_parts = [], []
    if n_lat:
        rows = n_lat // GRID_W
        row = jnp.repeat(jnp.arange(rows, dtype=F32), GRID_W)
        col = jnp.tile(jnp.arange(GRID_W, dtype=F32), rows)
        ang = jnp.concatenate([row[:, None] * inv_freq, col[:, None] * inv_freq], axis=-1)
        cos_parts.append(jnp.cos(ang))
        sin_parts.append(jnp.sin(ang))
    if n_ctx:
        cos_parts.append(jnp.ones((n_ctx, 32), F32))
        sin_parts.append(jnp.zeros((n_ctx, 32), F32))
    cos = jnp.concatenate(cos_parts, axis=0)
    sin = jnp.concatenate(sin_parts, axis=0)
    cos_f = jnp.tile(jnp.concatenate([cos, cos], axis=-1), (1, 2))
    sin_s = jnp.tile(jnp.concatenate([-sin, sin], axis=-1), (1, 2))
    return cos_f, sin_s


def _rope(x, cos_f, sin_s, first_half):
    swapped = jnp.where(first_half, pltpu.roll(x, 96, 1), pltpu.roll(x, 32, 1))
    return x * cos_f + swapped * sin_s


def _softmax_pv(q, k_scr, v_scr):
    s = _dot_nt(q, k_scr[...]).astype(BF16)
    m = jnp.max(s, axis=-1, keepdims=True)
    acc = _dot(jnp.exp2(s - m), v_scr[...])
    return acc[:, :LANE], acc[:, LANE:]


def _softmax_pv_t(q, k_scr, vt_scr):
    s = _dot_nt(k_scr[...], q).astype(BF16)
    m = jnp.max(s, axis=0, keepdims=True)
    acc = _dot(vt_scr[...], jnp.exp2(s - m))
    return acc[:LANE, :], acc[LANE:, :]


def _diff_attn_kernel(n_seg, seg_rows, tq, hps, *refs):
    lam_ref, q_ref = refs[0], refs[1]
    k_refs = refs[2:2 + n_seg]
    v_refs = refs[2 + n_seg:2 + 2 * n_seg]
    (gq_ref, gk_ref, go_ref, cq_ref, sq_ref, ck_ref, sk_ref, o_ref,
     k_scr, v_scr, q0_scr, q1_scr) = refs[2 + 2 * n_seg:]

    lane = lax.broadcasted_iota(jnp.int32, (1, LANE), 1)
    first_half = (lane % 64) < 32
    comp0 = lane < 64

    def head_norm(x, g):
        gi = lax.broadcasted_iota(jnp.int32, (LANE, LANE), 0) // 64
        gj = lax.broadcasted_iota(jnp.int32, (LANE, LANE), 1) // 64
        group_mean = jnp.where(gi == gj, 1.0 / 64.0, 0.0).astype(BF16)
        ms = _dot((x * x).astype(BF16), group_mean)
        return x * lax.rsqrt(ms + EPS) * g

    def prep(c0):
        r0 = 0
        for s in range(n_seg):
            n = seg_rows[s]
            k = head_norm(k_refs[s][:, c0:c0 + LANE].astype(F32), gk_ref[...])
            k = _rope(k, ck_ref[r0:r0 + n, :], sk_ref[r0:r0 + n, :], first_half)
            k_scr[r0:r0 + n, :] = k.astype(BF16)
            v_scr[r0:r0 + n, :LANE] = v_refs[s][:, c0:c0 + LANE]
            v_scr[r0:r0 + n, LANE:] = jnp.ones((n, LANE), BF16)
            r0 += n
        q = head_norm(q_ref[:, c0:c0 + LANE].astype(F32), gq_ref[...])
        q = _rope(q, cq_ref[...], sq_ref[...], first_half) * (64.0 ** -0.5 * LOG2E)
        q0_scr[...] = jnp.where(comp0, q, 0.0).astype(BF16)
        q1_scr[...] = jnp.where(comp0, 0.0, q).astype(BF16)

    sub = min(tq, SUB_DIFF)
    for hh in range(hps):
        c0 = hh * LANE
        pl.when(pl.program_id(2) == 0)(functools.partial(prep, c0))
        for j in range(tq // sub):
            row = pl.multiple_of(pl.program_id(2) * tq + j * sub, sub)
            o0, l0 = _softmax_pv(q0_scr[pl.ds(row, sub), :], k_scr, v_scr)
            o1, l1 = _softmax_pv(q1_scr[pl.ds(row, sub), :], k_scr, v_scr)
            o = o0 * (1.0 / l0) - o1 * (lam_ref[0] / l1)
            ms = jnp.mean(o * o, axis=-1, keepdims=True)
            o_ref[j * sub:(j + 1) * sub, c0:c0 + LANE] = (o * lax.rsqrt(ms + EPS) * go_ref[...]).astype(BF16)


def diff_attention(z, lam, gq, gk, go, cos_f, sin_s, n_q, q_blk0, segs, tq, hps=1):
    n_seg = len(segs)
    seg_rows = tuple(r for r, _ in segs)
    n_k = sum(seg_rows)
    tq = min(tq, n_q)
    nqb = n_q // tq
    w = hps * LANE
    hq, hk, hv = Z_DQ // w, Z_DK // w, Z_DV // w

    def kspec(rows, off, col0):
        return pl.BlockSpec((rows, w), lambda b, h, qb: (off + b, col0 + h))

    vec = pl.BlockSpec((1, LANE), lambda b, h, qb: (0, 0))
    qtab = pl.BlockSpec((n_q, LANE), lambda b, h, qb: (0, 0))
    ktab = pl.BlockSpec((n_k, LANE), lambda b, h, qb: (0, 0))
    in_specs = ([pl.BlockSpec(memory_space=pltpu.SMEM), kspec(n_q, q_blk0, hq)]
                + [kspec(r, off, hk) for r, off in segs]
                + [kspec(r, off, hv) for r, off in segs]
                + [vec, vec, vec, qtab, qtab, ktab, ktab])
    return pl.pallas_call(
        functools.partial(_diff_attn_kernel, n_seg, seg_rows, tq, hps),
        grid=(BATCH, HEADS // hps, nqb),
        in_specs=in_specs,
        out_specs=pl.BlockSpec((tq, w), lambda b, h, qb: (b * nqb + qb, h)),
        out_shape=jax.ShapeDtypeStruct((BATCH * n_q, HEADS * LANE), BF16),
        scratch_shapes=[pltpu.VMEM((n_k, LANE), BF16), pltpu.VMEM((n_k, 2 * LANE), BF16),
                        pltpu.VMEM((n_q, LANE), BF16), pltpu.VMEM((n_q, LANE), BF16)],
        compiler_params=_cp(3), name="diff_attention",
    )(lam, z, *([z] * (2 * n_seg)), gq, gk, go, cos_f, sin_s, cos_f, sin_s)


def _mla_up_kernel(cq_ref, ckv_ref, gq_ref, gkv_ref, wq_ref, wkv_ref, q_ref, kv_ref):
    def norm(x, g):
        ms = jnp.mean(x * x, axis=-1, keepdims=True)
        return (x * lax.rsqrt(ms + EPS) * g).astype(BF16)

    q_ref[...] = _dot(norm(cq_ref[...].astype(F32), gq_ref[...]), wq_ref[...]).astype(BF16)
    kv_ref[...] = _dot(norm(ckv_ref[...].astype(F32), gkv_ref[...]), wkv_ref[...]).astype(BF16)


def mla_up(z, g_cq, g_ckv, w_uq, w_ukv):
    tm = 1024
    n = HEADS * 256
    return pl.pallas_call(
        _mla_up_kernel,
        grid=(N_TOK // tm,),
        in_specs=[pl.BlockSpec((tm, MLA_Q_LORA), lambda i: (i, Z_CQ // MLA_Q_LORA)),
                  pl.BlockSpec((tm, MLA_KV_LORA), lambda i: (i, Z_CKV // MLA_KV_LORA)),
                  pl.BlockSpec((1, MLA_Q_LORA), lambda i: (0, 0)),
                  pl.BlockSpec((1, MLA_KV_LORA), lambda i: (0, 0)),
                  pl.BlockSpec((MLA_Q_LORA, n), lambda i: (0, 0)),
                  pl.BlockSpec((MLA_KV_LORA, n), lambda i: (0, 0))],
        out_specs=[pl.BlockSpec((tm, n), lambda i: (i, 0))] * 2,
        out_shape=[jax.ShapeDtypeStruct((N_TOK, n), BF16)] * 2,
        compiler_params=_cp(1), name="mla_up",
    )(z, z, g_cq, g_ckv, w_uq, w_ukv)


def _mla_attn_kernel(n_seg, seg_rows, tq, hps, *refs):
    q_ref = refs[0]
    kv_refs = refs[1:1 + n_seg]
    kr_refs = refs[1 + n_seg:1 + 2 * n_seg]
    gq_ref, gk_ref, cq_ref, sq_ref, ck_ref, sk_ref, o_ref, k_scr, vt_scr, q_scr = refs[1 + 2 * n_seg:]

    lane = lax.broadcasted_iota(jnp.int32, (1, LANE), 1)
    first_half = (lane % 64) < 32

    def qk_norm(nope, rope, g_ref, scale):
        ones = jnp.ones((LANE, LANE), BF16)
        ss = _dot((nope * nope).astype(BF16), ones) + _dot((rope * rope).astype(BF16), ones)
        inv = lax.rsqrt(ss * (1.0 / MLA_QK) + EPS) * scale
        return ((nope * inv * g_ref[:, :LANE]).astype(BF16), (rope * inv * g_ref[:, LANE:]).astype(BF16))

    def prep(c0):
        r0 = 0
        for s in range(n_seg):
            n = seg_rows[s]
            rope = _rope(kr_refs[s][...].astype(F32), ck_ref[r0:r0 + n, :], sk_ref[r0:r0 + n, :], first_half)
            kn, kr = qk_norm(kv_refs[s][:, c0:c0 + LANE].astype(F32), rope, gk_ref, 1.0)
            k_scr[r0:r0 + n, :LANE] = kn
            k_scr[r0:r0 + n, LANE:] = kr
            vt_scr[:LANE, r0:r0 + n] = kv_refs[s][:, c0 + LANE:c0 + 2 * LANE].astype(F32).T.astype(BF16)
            vt_scr[LANE:, r0:r0 + n] = jnp.ones((LANE, n), BF16)
            r0 += n
        rope = _rope(q_ref[:, c0 + LANE:c0 + 2 * LANE].astype(F32), cq_ref[...], sq_ref[...], first_half)
        qn, qr = qk_norm(q_ref[:, c0:c0 + LANE].astype(F32), rope, gq_ref, MLA_QK ** -0.5 * LOG2E)
        q_scr[:, :LANE] = qn
        q_scr[:, LANE:] = qr

    sub = min(tq, SUB_MLA)
    for hh in range(hps):
        pl.when(pl.program_id(2) == 0)(functools.partial(prep, hh * 2 * LANE))
        for j in range(tq // sub):
            row = pl.multiple_of(pl.program_id(2) * tq + j * sub, sub)
            o, l = _softmax_pv_t(q_scr[pl.ds(row, sub), :], k_scr, vt_scr)
            o_ref[j * sub:(j + 1) * sub, hh * LANE:(hh + 1) * LANE] = (o * (1.0 / l)).T.astype(BF16)


def mla_attention(q_mla, kv, z, gq, gk, cos_f, sin_s, n_q, q_blk0, segs, tq, hps=1):
    n_seg = len(segs)
    seg_rows = tuple(r for r, _ in segs)
    n_k = sum(seg_rows)
    tq = min(tq, n_q)
    nqb = n_q // tq
    ckr = Z_KR // LANE
    w = hps * 2 * LANE
    vec = pl.BlockSpec((1, 2 * LANE), lambda b, h, qb: (0, 0))
    qtab = pl.BlockSpec((n_q, LANE), lambda b, h, qb: (0, 0))
    ktab = pl.BlockSpec((n_k, LANE), lambda b, h, qb: (0, 0))
    in_specs = ([pl.BlockSpec((n_q, w), lambda b, h, qb: (q_blk0 + b, h))]
                + [pl.BlockSpec((r, w), (lambda off: lambda b, h, qb: (off + b, h))(off)) for r, off in segs]
                + [pl.BlockSpec((r, LANE), (lambda off: lambda b, h, qb: (off + b, ckr))(off)) for r, off in segs]
                + [vec, vec, qtab, qtab, ktab, ktab])
    return pl.pallas_call(
        functools.partial(_mla_attn_kernel, n_seg, seg_rows, tq, hps),
        grid=(BATCH, HEADS // hps, nqb),
        in_specs=in_specs,
        out_specs=pl.BlockSpec((tq, hps * LANE), lambda b, h, qb: (b * nqb + qb, h)),
        out_shape=jax.ShapeDtypeStruct((BATCH * n_q, HEADS * LANE), BF16),
        scratch_shapes=[pltpu.VMEM((n_k, 2 * LANE), BF16), pltpu.VMEM((2 * LANE, n_k), BF16),
                        pltpu.VMEM((n_q, 2 * LANE), BF16)],
        compiler_params=_cp(3), name="mla_attention",
    )(q_mla, *([kv] * n_seg), *([z] * n_seg), gq, gk, cos_f, sin_s, cos_f, sin_s)


def _merge_kernel(n_lat_tiles, n_src, *refs):
    srcs = [refs[3 * s:3 * s + 3] for s in range(n_src)]
    g0_ref, g1_ref, g2_ref, wf_ref, wd_ref, wm_ref, o_ref = refs[3 * n_src:]

    def body(yf_ref, od_ref, om_ref):
        f = _dot(yf_ref[...], wf_ref[...])
        d = _dot(od_ref[...], wd_ref[...])
        m = _dot(om_ref[...], wm_ref[...])
        o = (_sigmoid(g0_ref[...].astype(F32)) * f + _sigmoid(g1_ref[...].astype(F32)) * d
             + _sigmoid(g2_ref[...].astype(F32)) * m)
        o_ref[...] = o.astype(BF16)

    if n_src == 1:
        body(*srcs[0])
    else:
        i = pl.program_id(0)
        pl.when(i < n_lat_tiles)(lambda: body(*srcs[0]))
        pl.when(i >= n_lat_tiles)(lambda: body(*srcs[1]))


def merge_branches(lat, ctx, z, w_f, w_d, w_m):
    tm, tn = TM, 512
    k = FNET_W
    nl = N_LAT // tm
    n_rows = N_LAT + (N_CTX if ctx is not None else 0)
    w_spec = pl.BlockSpec((k, tn), lambda i, j: (0, j))
    g0 = Z_GATE // tn

    def gspec(r):
        return pl.BlockSpec((tm, tn), lambda i, j: (i, g0 + r * (D // tn) + j))

    lat_spec = pl.BlockSpec((tm, k), lambda i, j: (jnp.minimum(i, nl - 1), 0))
    ctx_spec = pl.BlockSpec((tm, k), lambda i, j: (jnp.maximum(i - nl, 0), 0))
    srcs = list(lat) + (list(ctx) if ctx is not None else [])
    specs = [lat_spec] * 3 + ([ctx_spec] * 3 if ctx is not None else [])
    return pl.pallas_call(
        functools.partial(_merge_kernel, nl, 2 if ctx is not None else 1),
        grid=(n_rows // tm, D // tn),
        in_specs=specs + [gspec(0), gspec(1), gspec(2), w_spec, w_spec, w_spec],
        out_specs=pl.BlockSpec((tm, tn), lambda i, j: (i, j)),
        out_shape=jax.ShapeDtypeStruct((n_rows, D), BF16),
        compiler_params=_cp(2), name="merge_branches",
    )(*srcs, z, z, z, w_f, w_d, w_m)


def _split3(x):
    hi = x.astype(BF16)
    r = x - hi.astype(F32)
    mid = r.astype(BF16)
    lo = (r - mid.astype(F32)).astype(BF16)
    return hi, mid, lo


HALF = D // 2
HI16 = 0xFFFF0000


def _pack_bf16_pair(x):
    lo = lax.bitcast_convert_type(x[:, :HALF].astype(BF16).astype(F32), jnp.uint32) >> 16
    hi = lax.bitcast_convert_type(x[:, HALF:].astype(BF16).astype(F32), jnp.uint32) & jnp.uint32(HI16)
    return lo | hi


def _unpack_bf16_pair(p):
    lo = lax.bitcast_convert_type(p << 16, F32).astype(BF16)
    hi = lax.bitcast_convert_type(p & jnp.uint32(HI16), F32).astype(BF16)
    return lo, hi


def _router_kernel(x_ref, g_ref, sh_ref, sc_ref, wh_ref, wm_ref, wl_ref, u_ref, rt_ref):
    u = _modulate(x_ref[...], g_ref[...], sh_ref[0], sc_ref[0])
    u_ref[...] = _pack_bf16_pair(u)
    uh, um, ul = _split3(u)
    wh, wm, wl = wh_ref[...], wm_ref[...], wl_ref[...]
    logits = (_dot(ul, wh) + _dot(uh, wl) + _dot(um, wm)) + (_dot(um, wh) + _dot(uh, wm)) + _dot(uh, wh)
    lane = lax.broadcasted_iota(jnp.int32, logits.shape, 1).astype(F32)
    neg = -jnp.inf
    lg = jnp.where(lane < N_EXPERTS, logits, neg)
    m1 = jnp.max(lg, axis=-1, keepdims=True)
    i1 = jnp.min(jnp.where(lg == m1, lane, float(LANE)), axis=-1, keepdims=True)
    lg2 = jnp.where(lane == i1, neg, lg)
    m2 = jnp.max(lg2, axis=-1, keepdims=True)
    i2 = jnp.min(jnp.where(lg2 == m2, lane, float(LANE)), axis=-1, keepdims=True)
    p1 = 1.0 / (1.0 + jnp.exp(m2 - m1))
    p2 = 1.0 - p1
    rt = jnp.where(lane == 0.0, i1, jnp.where(lane == 1.0, i2, jnp.where(lane == 2.0, p1,
                                                                      jnp.where(lane == 3.0, p2, 0.0))))
    rt_ref[...] = rt


def moe_router(x, g, mod, chunk, w_router, n_rows):
    tm = 512
    row = _mod_row(tm)
    wpad = jnp.pad(w_router, ((0, 0), (0, LANE - N_EXPERTS)))
    wh, wm, wl = _split3(wpad)
    wspec = pl.BlockSpec((D, LANE), lambda i: (0, 0))
    return pl.pallas_call(
        _router_kernel,
        grid=(n_rows // tm,),
        in_specs=[pl.BlockSpec((tm, D), lambda i: (i, 0)),
                  pl.BlockSpec((1, D), lambda i: (0, 0)),
                  pl.BlockSpec((1, 1, D), lambda i: (row(i), 0, chunk)),
                  pl.BlockSpec((1, 1, D), lambda i: (row(i), 0, chunk + 1)),
                  wspec, wspec, wspec],
        out_specs=[pl.BlockSpec((tm, HALF), lambda i: (i, 0)), pl.BlockSpec((tm, LANE), lambda i: (i, 0))],
        out_shape=[jax.ShapeDtypeStruct((n_rows, HALF), jnp.uint32),
                   jax.ShapeDtypeStruct((n_rows, LANE), F32)],
        compiler_params=_cp(1), name="moe_router",
    )(x, g, mod, mod, wh, wm, wl)


def _row_copy(src_hbm, dst_ref, sem, src_row, dst_row):
    return pltpu.make_async_copy(src_hbm.at[pl.ds(src_row, 1), :], dst_ref.at[pl.ds(dst_row, 1), :], sem)


def _gather_kernel(idx_ref, src_hbm, o_ref, sem):
    tg = o_ref.shape[0]
    base = pl.program_id(0) * tg

    def issue(g, c):
        for u in range(DMA_GROUP):
            r = g * DMA_GROUP + u
            _row_copy(src_hbm, o_ref, sem, idx_ref[base + r], r).start(priority=u % 2)
        return c

    lax.fori_loop(0, tg // DMA_GROUP, issue, 0)

    def wait(g, c):
        for u in range(DMA_GROUP):
            _row_copy(src_hbm, o_ref, sem, 0, g * DMA_GROUP + u).wait()
        return c

    lax.fori_loop(0, tg // DMA_GROUP, wait, 0)


def gather_rows(src, idx):
    n_out = idx.shape[0]
    return pl.pallas_call(
        _gather_kernel,
        grid_spec=pltpu.PrefetchScalarGridSpec(
            num_scalar_prefetch=1, grid=(n_out // TG_GATHER,),
            in_specs=[pl.BlockSpec(memory_space=pl.ANY)],
            out_specs=pl.BlockSpec((TG_GATHER, src.shape[1]), lambda i, idx: (i, 0)),
            scratch_shapes=[pltpu.SemaphoreType.DMA]),
        out_shape=jax.ShapeDtypeStruct((n_out, src.shape[1]), src.dtype),
        compiler_params=_cp(1), name="gather_rows",
    )(idx, src)


def _moe_kernel(te_ref, nu_ref, xs_ref, w1_ref, w3_ref, w2_ref, o_ref, xb_scr, acc_scr):
    i, j = pl.program_id(0), pl.program_id(1)
    last = pl.num_programs(1) - 1
    used = i < nu_ref[0]

    @pl.when(used)
    def _():
        @pl.when(j == 0)
        def _():
            lo, hi = _unpack_bf16_pair(xs_ref[...])
            xb_scr[:, :HALF] = lo
            xb_scr[:, HALF:] = hi
            acc_scr[...] = jnp.zeros_like(acc_scr)

        xb = xb_scr[...]
        a = _dot(xb, w1_ref[0])
        b = _dot(xb, w3_ref[0])
        h = (a * _sigmoid(a) * b).astype(BF16)
        acc_scr[...] += _dot(h, w2_ref[0])

        @pl.when(j == last)
        def _():
            o_ref[...] = acc_scr[...]

    @pl.when(jnp.logical_and(jnp.logical_not(used), j == last))
    def _():
        o_ref[...] = jnp.zeros_like(o_ref)


def moe_experts(xs, tile_expert, n_used, w1, w3, w2):
    p = xs.shape[0]
    tm, tf = TM_MOE, TF_MOE
    return pl.pallas_call(
        _moe_kernel,
        grid_spec=pltpu.PrefetchScalarGridSpec(
            num_scalar_prefetch=2, grid=(p // tm, D_FF_EXPERT // tf),
            in_specs=[pl.BlockSpec((tm, HALF), lambda i, j, te, nu: (i, 0)),
                      pl.BlockSpec((1, D, tf), lambda i, j, te, nu: (te[i], 0, j)),
                      pl.BlockSpec((1, D, tf), lambda i, j, te, nu: (te[i], 0, j)),
                      pl.BlockSpec((1, tf, D), lambda i, j, te, nu: (te[i], j, 0))],
            out_specs=pl.BlockSpec((tm, D), lambda i, j, te, nu: (i, 0)),
            scratch_shapes=[pltpu.VMEM((tm, D), BF16), pltpu.VMEM((tm, D), F32)]),
        out_shape=jax.ShapeDtypeStruct((p, D), F32),
        compiler_params=_cp(2), name="moe_experts",
    )(tile_expert, n_used, xs, w1, w3, w2)


def _combine_kernel(n_tok, pos_ref, ys_hbm, x_ref, rt_ref, gate_ref, o_ref, buf, sem):
    tg = o_ref.shape[0]
    base = pl.program_id(0) * tg

    def issue(g, c):
        for u in range(DMA_GROUP):
            r = g * DMA_GROUP + u
            _row_copy(ys_hbm, buf.at[0], sem, pos_ref[base + r], r).start(priority=0)
            _row_copy(ys_hbm, buf.at[1], sem, pos_ref[n_tok + base + r], r).start(priority=1)
        return c

    lax.fori_loop(0, tg // DMA_GROUP, issue, 0)

    def wait(g, c):
        for u in range(DMA_GROUP):
            r = g * DMA_GROUP + u
            _row_copy(ys_hbm, buf.at[0], sem, 0, r).wait()
            _row_copy(ys_hbm, buf.at[1], sem, 0, r).wait()
        return c

    lax.fori_loop(0, tg // DMA_GROUP, wait, 0)
    rt = rt_ref[...]
    y = rt[:, 2:3] * buf[0] + rt[:, 3:4] * buf[1]
    o_ref[...] = x_ref[...] + gate_ref[0] * y


def moe_combine(ys, pos, x, rt, mod, chunk, n_rows):
    tg = TG_COMBINE
    row = _mod_row(tg)
    return pl.pallas_call(
        functools.partial(_combine_kernel, n_rows),
        grid_spec=pltpu.PrefetchScalarGridSpec(
            num_scalar_prefetch=1, grid=(n_rows // tg,),
            in_specs=[pl.BlockSpec(memory_space=pl.ANY),
                      pl.BlockSpec((tg, D), lambda i, pos: (i, 0)),
                      pl.BlockSpec((tg, LANE), lambda i, pos: (i, 0)),
                      pl.BlockSpec((1, 1, D), lambda i, pos: (row(i), 0, chunk))],
            out_specs=pl.BlockSpec((tg, D), lambda i, pos: (i, 0)),
            scratch_shapes=[pltpu.VMEM((2, tg, D), F32), pltpu.SemaphoreType.DMA]),
        out_shape=jax.ShapeDtypeStruct((n_rows, D), F32),
        compiler_params=_cp(1), name="moe_combine",
    )(pos, ys, x, rt, mod)


def moe_ffn(x, g, mod, w_router, w1, w3, w2, n_rows):
    u, rt = moe_router(x, g, mod, 3, w_router, n_rows)
    tm = TM_MOE
    n_pairs = 2 * n_rows
    p_rows = n_pairs + N_EXPERTS * tm
    n_tiles = p_rows // tm
    e_flat = jnp.concatenate([rt[:, 0], rt[:, 1]]).astype(jnp.int32)
    onehot = (e_flat[:, None] == jnp.arange(N_EXPERTS, dtype=jnp.int32)[None, :]).astype(jnp.int32)
    csum = jnp.cumsum(onehot, axis=0)
    counts = csum[-1]
    rank = jnp.sum((csum - onehot) * onehot, axis=1)
    padded = ((counts + tm - 1) // tm) * tm
    g_end = jnp.cumsum(padded)
    g_start = g_end - padded
    dest = g_start[e_flat] + rank
    tok = jnp.tile(jnp.arange(n_rows, dtype=jnp.int32), 2)
    src = jnp.zeros((p_rows,), jnp.int32).at[dest].set(tok)
    tile_start = jnp.arange(n_tiles, dtype=jnp.int32) * tm
    tile_expert = jnp.minimum(jnp.sum((tile_start[:, None] >= g_end[None, :]).astype(jnp.int32), axis=1),
                              N_EXPERTS - 1).astype(jnp.int32)
    n_used = (g_end[-1] // tm).astype(jnp.int32).reshape(1)
    xs = gather_rows(u, src)
    ys = moe_experts(xs, tile_expert, n_used, w1, w3, w2)
    return moe_combine(ys, dest, x, rt, mod, 5, n_rows)


def kernel(x, c, ctx, c_ctx, w_mod, b_mod, g_norm1, g_norm2, w_in, g_diff_q, g_diff_k, diff_lambda,
           g_diff_out, w_diff_o, g_mla_cq, g_mla_ckv, w_mla_uq, w_mla_ukv, g_mla_q, g_mla_k, w_mla_o,
           w_fnet, w_out, ffn_w1, ffn_w3, ffn_w2, moe_router, moe_w1, moe_w3, moe_w2):
    xs = jnp.concatenate([x.reshape(N_LAT, D), ctx.reshape(N_CTX, D)], axis=0)
    c_all = jnp.zeros((32, D), F32).at[:BATCH].set(c).at[BATCH].set(c_ctx)
    mods = mod_tables(c_all, w_mod, b_mod)

    cos_lat, sin_lat = rope_tables(SEQ, CTX)
    cos_ctx, sin_ctx = rope_tables(0, CTX)
    cw, sw = dft_matrices(FNET_GROUP_W, FNET_GROUP_W ** -0.5)
    cn_lat, sn_lat = dft_matrices(SEQ, SEQ ** -0.5)
    cn_ctx, sn_ctx = dft_matrices(CTX, CTX ** -0.5)
    sn_lat, sn_ctx = -sn_lat, -sn_ctx

    lat_segs = [(SEQ, 0), (CTX, N_LAT // CTX)]
    ctx_segs = [(CTX, N_LAT // CTX)]

    for i in range(DEPTH):
        need_ctx = i < DEPTH - 1
        n_rows = N_TOK if need_ctx else N_LAT
        mod = mods[i].reshape(32, 1, N_MOD * D)
        wi = w_in[i]
        w_in_p = jnp.concatenate(
            [wi[:, :Z_KR], wi[:, Z_KR:Z_KR + MLA_ROPE], jnp.zeros((D, Z_GATE - Z_KR - MLA_ROPE), F32),
             wi[:, Z_KR + MLA_ROPE:]], axis=1).astype(BF16)
        w_uq = jnp.pad(w_mla_uq[i].reshape(MLA_Q_LORA, HEADS, MLA_QK),
                       ((0, 0), (0, 0), (0, 256 - MLA_QK))).reshape(MLA_Q_LORA, HEADS * 256).astype(BF16)
        w_ukv = w_mla_ukv[i].astype(BF16)
        g_q = jnp.pad(g_mla_q[i], (0, 256 - MLA_QK)).reshape(1, 256)
        g_k = jnp.pad(g_mla_k[i], (0, 256 - MLA_QK)).reshape(1, 256)
        lam_init = 0.8 - 0.6 * math.exp(-0.3 * i)
        lp = diff_lambda[i]
        lam = (jnp.exp(jnp.sum(lp[0] * lp[1])) - jnp.exp(jnp.sum(lp[2] * lp[3])) + lam_init).reshape(1)
        gdq = jnp.tile(g_diff_q[i], 2).reshape(1, LANE)
        gdk = jnp.tile(g_diff_k[i], 2).reshape(1, LANE)
        gdo = (g_diff_out[i] * (1.0 - lam_init)).reshape(1, LANE)

        z = norm_matmul(xs, g_norm1[i].reshape(1, D), mod, 0, [w_in_p], 1024, N_TOK)
        q_mla, kv = mla_up(z, g_mla_cq[i].reshape(1, -1), g_mla_ckv[i].reshape(1, -1), w_uq, w_ukv)
        lat = (fourier_mix(cn_lat, sn_lat, z, cw, sw, SEQ, 0, min(512, SEQ)),
               diff_attention(z, lam, gdq, gdk, gdo, cos_lat, sin_lat, SEQ, 0, lat_segs, TQ_DIFF),
               mla_attention(q_mla, kv, z, g_q, g_k, cos_lat, sin_lat, SEQ, 0, lat_segs, TQ_MLA))
        ctx_out = None
        if need_ctx:
            c0 = N_LAT // CTX
            ctx_out = (fourier_mix(cn_ctx, sn_ctx, z, cw, sw, CTX, c0, CTX),
                       diff_attention(z, lam, gdq, gdk, gdo, cos_ctx, sin_ctx, CTX, c0, ctx_segs, CTX, HEADS),
                       mla_attention(q_mla, kv, z, g_q, g_k, cos_ctx, sin_ctx, CTX, c0, ctx_segs, CTX, HEADS))
        merged = merge_branches(lat, ctx_out, z, w_fnet[i].astype(BF16), w_diff_o[i].astype(BF16),
                                w_mla_o[i].astype(BF16))
        xs = matmul_residual(merged, w_out[i].astype(BF16), xs, mod, 2, TM, 1024, n_rows)

        g2 = g_norm2[i].reshape(1, D)
        if i % 2 == 0:
            j = i // 2
            h = norm_matmul(xs, g2, mod, 3, [ffn_w1[j].astype(BF16), ffn_w3[j].astype(BF16)], 512, n_rows)
            xs = matmul_residual(h, ffn_w2[j].astype(BF16), xs, mod, 5, TM_FFN2, 512, n_rows)
        else:
            j = i // 2
            xs = moe_ffn(xs, g2, mod, moe_router[j], moe_w1[j].astype(BF16), moe_w3[j].astype(BF16),
                         moe_w2[j].astype(BF16), n_rows)
    return xs[:N_LAT].reshape(BATCH, SEQ, D)
```
